```python
import math
import jax, jax.numpy as jnp
from jax import lax
import numpy as np

D_MODEL = 1024
BATCH = 8
SEQ = 2048
DEPTH = 4

CHUNK = 64
Q_BLOCK = 128
N_MEM = 256
EPS = 1e-6
NEG_INF = -1e30
ROPE_THETA = 500000.0
RET_THETA = 10000.0

RET_HEADS = 4
RET_DK = 64
RET_DV = 128
MLA_HEADS = 8
MLA_Q_RANK = 256
MLA_KV_RANK = 128
MLA_NOPE = 64
MLA_ROPE = 32
MLA_DV = 64
DIFF_HEADS = 4
DIFF_HD = 64
DIFF_ROT = DIFF_HD // 4
N_BRANCH = 3
BRANCH_WIDTH = 512
CROSS_HEADS = 4
CROSS_HD = D_MODEL // CROSS_HEADS
D_FF = 2816

IN_SPLITS = (
    RET_HEADS * RET_DK,
    RET_HEADS * RET_DK,
    RET_HEADS * RET_DV,
    RET_HEADS * RET_DV,
    MLA_Q_RANK,
    MLA_KV_RANK + MLA_ROPE,
    2 * DIFF_HEADS * DIFF_HD,
    2 * DIFF_HEADS * DIFF_HD,
    2 * DIFF_HEADS * DIFF_HD,
    N_BRANCH * D_MODEL,
)
IN_WIDTH = sum(IN_SPLITS)

kernel_name = 'hybrid_streaming_encoder_block'


def rms_norm(x, gain=None):
    xf = x.astype(jnp.float32)
    y = xf * lax.rsqrt(jnp.mean(xf * xf, axis=-1, keepdims=True) + EPS)
    if gain is not None:
        y = y * gain.astype(jnp.float32)
    return y.astype(x.dtype)


def rope_angles(positions, dim, theta):
    inv = 1.0 / (theta ** (jnp.arange(0, dim, 2, dtype=jnp.float32) / dim))
    ang = positions.astype(jnp.float32)[..., None] * inv
    return jnp.cos(ang), jnp.sin(ang)


def apply_rope(x, cos, sin):
    half = x.shape[-1] // 2
    x1, x2 = x[..., :half], x[..., half:]
    c = cos.astype(x.dtype)
    s = sin.astype(x.dtype)
    return jnp.concatenate([x1 * c - x2 * s, x2 * c + x1 * s], axis=-1)


def partial_rope(x, cos, sin, rot):
    return jnp.concatenate([apply_rope(x[..., :rot], cos, sin), x[..., rot:]], axis=-1)


def to_blocks(t):
    b, s = t.shape[:2]
    return jnp.moveaxis(t.reshape((b, s // Q_BLOCK, Q_BLOCK) + t.shape[2:]), 1, 0)


def from_blocks(t):
    t = jnp.moveaxis(t, 0, 1)
    return t.reshape((t.shape[0], t.shape[1] * t.shape[2]) + t.shape[3:])


def chunk_causal_mask(block_idx, seq):
    q_pos = block_idx * Q_BLOCK + jnp.arange(Q_BLOCK)
    k_pos = jnp.arange(seq)
    return (k_pos[None, :] // CHUNK) <= (q_pos[:, None] // CHUNK)


def masked_softmax(scores, mask):
    s = jnp.where(mask, scores.astype(jnp.float32), NEG_INF)
    return jax.nn.softmax(s, axis=-1)


def swiglu(h, w13, w2):
    gate, up = jnp.split(h @ w13, 2, axis=-1)
    return (jax.nn.silu(gate) * up) @ w2


def retention(q, k, v, g):
    b, s, h, dk = q.shape
    nc = s // CHUNK
    dt = q.dtype
    q = q * (dk ** -0.5)
    log_gamma = jnp.log(1.0 - 2.0 ** (-5.0 - jnp.arange(h, dtype=jnp.float32)))
    idx = jnp.arange(CHUNK, dtype=jnp.float32)
    intra_decay = jnp.exp(log_gamma[:, None, None] * jnp.abs(idx[:, None] - idx[None, :]))
    q_decay = jnp.exp(log_gamma[None, :] * (idx[:, None] + 1.0))
    k_decay = jnp.exp(log_gamma[None, :] * (CHUNK - 1.0 - idx[:, None]))
    chunk_decay = jnp.exp(log_gamma * CHUNK).astype(dt)
    qc = q.reshape(b, nc, CHUNK, h, dk)
    kc = k.reshape(b, nc, CHUNK, h, dk)
    vc = v.reshape(b, nc, CHUNK, h, -1)
    scores = jnp.einsum('bcnhd,bcmhd->bchnm', qc, kc) * intra_decay.astype(dt)
    o_intra = jnp.einsum('bchnm,bcmhe->bcnhe', scores, vc)
    kv = jnp.einsum('bcmhd,bcmhe->bchde', kc * k_decay.astype(dt)[:, :, None], vc)

    def step(state, kv_c):
        return state * chunk_decay[None, :, None, None] + kv_c, state

    _, s_prev = lax.scan(step, jnp.zeros_like(kv[:, 0]), jnp.moveaxis(kv, 1, 0))
    s_prev = jnp.moveaxis(s_prev, 0, 1)
    o_cross = jnp.einsum('bcnhd,bchde->bcnhe', qc * q_decay.astype(dt)[:, :, None], s_prev)
    o = rms_norm(o_intra + o_cross).reshape(b, s, -1)
    return jax.nn.silu(g) * o


def latent_attention(c_q_raw, kv_a, q_norm, kv_norm, wq_b, wkv_b, cos, sin):
    b, s, _ = c_q_raw.shape
    q = (rms_norm(c_q_raw, q_norm) @ wq_b).reshape(b, s, MLA_HEADS, MLA_NOPE + MLA_ROPE)
    q_nope = q[..., :MLA_NOPE]
    q_rope = apply_rope(q[..., MLA_NOPE:], cos[:, :, None], sin[:, :, None])
    c_kv, k_rope = kv_a[..., :MLA_KV_RANK], kv_a[..., MLA_KV_RANK:]
    k_rope = apply_rope(k_rope, cos, sin)
    kv = (rms_norm(c_kv, kv_norm) @ wkv_b).reshape(b, s, MLA_HEADS, MLA_NOPE + MLA_DV)
    k_nope, v = kv[..., :MLA_NOPE], kv[..., MLA_NOPE:]
    scale = (MLA_NOPE + MLA_ROPE) ** -0.5

    def block(args):
        qn, qr, i = args
        sc = (jnp.einsum('bqhd,bkhd->bhqk', qn, k_nope)
              + jnp.einsum('bqhd,bkd->bhqk', qr, k_rope)) * scale
        p = masked_softmax(sc, chunk_causal_mask(i, s))
        return jnp.einsum('bhqk,bkhe->bqhe', p.astype(v.dtype), v)

    o = lax.map(block, (to_blocks(q_nope), to_blocks(q_rope), jnp.arange(s // Q_BLOCK)))
    return from_blocks(o).reshape(b, s, -1)


def diff_attention(q, k, v, lam_params, lambda_init, cos, sin):
    b, s, _ = q.shape
    q = q.reshape(b, s, DIFF_HEADS, 2, DIFF_HD)
    k = k.reshape(b, s, DIFF_HEADS, 2, DIFF_HD)
    v = v.reshape(b, s, DIFF_HEADS, 2 * DIFF_HD)
    c, sn = cos[:, :, None, None], sin[:, :, None, None]
    q = partial_rope(q, c, sn, DIFF_ROT)
    k = partial_rope(k, c, sn, DIFF_ROT)
    lp = lam_params.astype(jnp.float32)
    lam = jnp.exp(jnp.sum(lp[0] * lp[1])) - jnp.exp(jnp.sum(lp[2] * lp[3])) + lambda_init
    scale = DIFF_HD ** -0.5

    def block(args):
        qb, i = args
        sc = jnp.einsum('bqhjd,bkhjd->bhjqk', qb, k) * scale
        p = masked_softmax(sc, chunk_causal_mask(i, s))
        w = p[:, :, 0] - lam * p[:, :, 1]
        return jnp.einsum('bhqk,bkhe->bqhe', w.astype(v.dtype), v)

    o = from_blocks(lax.map(block, (to_blocks(q), jnp.arange(s // Q_BLOCK))))
    o = rms_norm(o) * (1.0 - lambda_init)
    return o.reshape(b, s, -1)


def cross_attention(h, mem_n, wq, wkv, wo):
    b, s, _ = h.shape
    q = (h @ wq).reshape(b, s, CROSS_HEADS, CROSS_HD)
    kv = (mem_n @ wkv).reshape(b, mem_n.shape[1], 2, CROSS_HEADS, CROSS_HD)
    k, v = kv[:, :, 0], kv[:, :, 1]
    sc = jnp.einsum('bqhd,bkhd->bhqk', q, k) * (CROSS_HD ** -0.5)
    p = jax.nn.softmax(sc.astype(jnp.float32), axis=-1).astype(v.dtype)
    o = jnp.einsum('bhqk,bkhe->bqhe', p, v).reshape(b, s, D_MODEL)
    return o @ wo


def token_mixing(h, w_in, mla_q_norm, mla_kv_norm, mla_wq_b, mla_wkv_b, diff_lambda,
                 w_branch, w_out, lambda_init, ret_rope, mla_rope, diff_rope):
    b, s, _ = h.shape
    split_points = [int(i) for i in np.cumsum(IN_SPLITS)[:-1]]
    (ret_q, ret_k, ret_v, ret_g, mla_qa, mla_kva,
     d_q, d_k, d_v, gate_logits) = jnp.split(h @ w_in, split_points, axis=-1)
    rc, rs = ret_rope[0][:, :, None], ret_rope[1][:, :, None]
    y_ret = retention(apply_rope(ret_q.reshape(b, s, RET_HEADS, RET_DK), rc, rs),
                      apply_rope(ret_k.reshape(b, s, RET_HEADS, RET_DK), rc, rs),
                      ret_v.reshape(b, s, RET_HEADS, RET_DV), ret_g)
    y_mla = latent_attention(mla_qa, mla_kva, mla_q_norm, mla_kv_norm, mla_wq_b, mla_wkv_b,
                             mla_rope[0], mla_rope[1])
    y_diff = diff_attention(d_q, d_k, d_v, diff_lambda, lambda_init, diff_rope[0], diff_rope[1])
    branches = jnp.stack([y_ret, y_mla, y_diff], axis=2)
    proj = jnp.einsum('bsnw,nwd->bsnd', branches, w_branch)
    gates = jax.nn.sigmoid(gate_logits.reshape(b, s, N_BRANCH, D_MODEL))
    return jnp.sum(gates * proj, axis=2) @ w_out


def setup_inputs(seed: int = 0) -> dict:
    key = jax.random.key(seed)
    ks = jax.random.split(key, 24)
    f32 = jnp.float32

    def dense(k, shape, fan_in):
        return jax.random.normal(k, shape, f32) * (fan_in ** -0.5)

    def gains(k, shape):
        return 1.0 + 0.05 * jax.random.normal(k, shape, f32)

    offsets = jax.random.randint(ks[2], (BATCH, 1), 0, 64) * CHUNK
    positions = (offsets + jnp.arange(SEQ)[None, :]).astype(jnp.int32)
    return {
        'x': jax.random.normal(ks[0], (BATCH, SEQ, D_MODEL), f32),
        'mem': jax.random.normal(ks[1], (BATCH, N_MEM, D_MODEL), f32),
        'positions': positions,
        'ffn1_norms': gains(ks[3], (DEPTH, 2, D_MODEL)),
        'ffn1_w13': dense(ks[4], (DEPTH, D_MODEL, 2 * D_FF), D_MODEL),
        'ffn1_w2': dense(ks[5], (DEPTH, D_FF, D_MODEL), D_FF),
        'mix_norms': gains(ks[6], (DEPTH, 2, D_MODEL)),
        'w_in': dense(ks[7], (DEPTH, D_MODEL, IN_WIDTH), D_MODEL),
        'mla_q_norm': gains(ks[8], (DEPTH, MLA_Q_RANK)),
        'mla_kv_norm': gains(ks[9], (DEPTH, MLA_KV_RANK)),
        'mla_wq_b': dense(ks[10], (DEPTH, MLA_Q_RANK, MLA_HEADS * (MLA_NOPE + MLA_ROPE)), MLA_Q_RANK),
        'mla_wkv_b': dense(ks[11], (DEPTH, MLA_KV_RANK, MLA_HEADS * (MLA_NOPE + MLA_DV)), MLA_KV_RANK),
        'diff_lambda': 0.1 * jax.random.normal(ks[12], (DEPTH, 4, DIFF_HD), f32),
        'w_branch': dense(ks[13], (DEPTH, N_BRANCH, BRANCH_WIDTH, D_MODEL), BRANCH_WIDTH),
        'w_out': dense(ks[14], (DEPTH, D_MODEL, D_MODEL), D_MODEL),
        'cross_norms': gains(ks[15], (DEPTH, 3, D_MODEL)),
        'cross_wq': dense(ks[16], (DEPTH, D_MODEL, D_MODEL), D_MODEL),
        'cross_wkv': dense(ks[17], (DEPTH, D_MODEL, 2 * D_MODEL), D_MODEL),
        'cross_wo': dense(ks[18], (DEPTH, D_MODEL, D_MODEL), D_MODEL),
        'ffn2_norms': gains(ks[19], (DEPTH, 2, D_MODEL)),
        'ffn2_w13': dense(ks[20], (DEPTH, D_MODEL, 2 * D_FF), D_MODEL),
        'ffn2_w2': dense(ks[21], (DEPTH, D_FF, D_MODEL), D_FF),
    }


def reference(x, mem, positions, ffn1_norms, ffn1_w13, ffn1_w2, mix_norms, w_in,
              mla_q_norm, mla_kv_norm, mla_wq_b, mla_wkv_b, diff_lambda, w_branch, w_out,
              cross_norms, cross_wq, cross_wkv, cross_wo, ffn2_norms, ffn2_w13, ffn2_w2):
    ret_rope = rope_angles(positions, RET_DK, RET_THETA)
    mla_rope = rope_angles(positions, MLA_ROPE, ROPE_THETA)
    diff_rope = rope_angles(positions, DIFF_ROT, ROPE_THETA)
    for l in range(DEPTH):
        lambda_init = 0.8 - 0.6 * math.exp(-0.3 * l)
        h = rms_norm(x, ffn1_norms[l, 0])
        x = x + 0.5 * rms_norm(swiglu(h, ffn1_w13[l], ffn1_w2[l]), ffn1_norms[l, 1])
        h = rms_norm(x, mix_norms[l, 0])
        y = token_mixing(h, w_in[l], mla_q_norm[l], mla_kv_norm[l], mla_wq_b[l], mla_wkv_b[l],
                         diff_lambda[l], w_branch[l], w_out[l], lambda_init,
                         ret_rope, mla_rope, diff_rope)
        x = x + rms_norm(y, mix_norms[l, 1])
        h = rms_norm(x, cross_norms[l, 0])
        m = rms_norm(mem, cross_norms[l, 2])
        x = x + rms_norm(cross_attention(h, m, cross_wq[l], cross_wkv[l], cross_wo[l]), cross_norms[l, 1])
        h = rms_norm(x, ffn2_norms[l, 0])
        x = x + 0.5 * rms_norm(swiglu(h, ffn2_w13[l], ffn2_w2[l]), ffn2_norms[l, 1])
    return x
```

```python
import functools
import math

import numpy as np
import jax
import jax.numpy as jnp
from jax import lax
from jax.experimental import pallas as pl
from jax.experimental.pallas import tpu as pltpu

F32 = jnp.float32
BF16 = jnp.bfloat16

D_MODEL = 1024
DEPTH = 4
CHUNK = 64
EPS = 1e-6
NEG_INF = -1e30
ROPE_THETA = 500000.0
RET_THETA = 10000.0

RET_HEADS = 4
RET_DK = 64
RET_DV = 128
MLA_HEADS = 8
MLA_Q_RANK = 256
MLA_KV_RANK = 128
MLA_NOPE = 64
MLA_ROPE = 32
MLA_DV = 64
DIFF_HEADS = 4
DIFF_HD = 64
DIFF_ROT = 16
N_BRANCH = 3
BRANCH_WIDTH = 512
CROSS_HEADS = 4
CROSS_HD = 256
D_FF = 2816
GATE_OFFSET = 3488

LANES = 128
VMEM_LIMIT = 56 * 1024 * 1024

TOKEN_TILE = 512
FF_TILE = D_FF // 2
ATTN_TILE = 256
RET_TILE = 256

RET_LOG_GAMMA = tuple(math.log(1.0 - 2.0 ** (-5.0 - h)) for h in range(RET_HEADS))


def _params(*sem):
    return pltpu.CompilerParams(dimension_semantics=sem, vmem_limit_bytes=VMEM_LIMIT)


def _rms(x, gain=None):
    y = x * lax.rsqrt(jnp.mean(x * x, axis=-1, keepdims=True) + EPS)
    return y if gain is None else y * gain


def _dot(a, b):
    return jnp.dot(a, b, preferred_element_type=F32)


def _dot_nt(a, b):
    return lax.dot_general(a, b, (((1,), (1,)), ((), ())), preferred_element_type=F32)


def _rope(x, cos, sin_signed, half):
    lane = lax.broadcasted_iota(jnp.int32, x.shape, 1)
    first = (lane & (2 * half - 1)) < half
    partner = jnp.where(first, pltpu.roll(x, LANES - half, 1), pltpu.roll(x, half, 1))
    return x * cos + partner * sin_signed


def _chunk_bias(n):
    row = lax.broadcasted_iota(jnp.int32, (n, n), 0)
    col = lax.broadcasted_iota(jnp.int32, (n, n), 1)
    return jnp.where((col >> 6) <= (row >> 6), 0.0, NEG_INF).astype(F32)


def _rope_patterns():
    inv = np.zeros((3, LANES), np.float32)
    sign = np.zeros((3, LANES), np.float32)
    for i in range(LANES):
        j = i % RET_DK
        inv[0, i] = 1.0 / (RET_THETA ** (np.float32(2 * (j % 32)) / RET_DK))
        sign[0, i] = -1.0 if j < 32 else 1.0
        if 64 <= i < 96:
            j = i - 64
            inv[1, i] = 1.0 / (ROPE_THETA ** (np.float32(2 * (j % 16)) / MLA_ROPE))
            sign[1, i] = -1.0 if j < 16 else 1.0
        j = i % DIFF_HD
        if j < DIFF_ROT:
            inv[2, i] = 1.0 / (ROPE_THETA ** (np.float32(2 * (j % 8)) / DIFF_ROT))
            sign[2, i] = -1.0 if j < 8 else 1.0
    return jnp.asarray(inv), jnp.asarray(sign)


def _rope_table_kernel(pos_ref, inv_ref, sign_ref, cos_ref, sin_ref):
    pos = pos_ref[...].astype(F32)
    for t in range(3):
        ang = pos * inv_ref[t:t + 1, :]
        cos_ref[t] = jnp.cos(ang)
        sin_ref[t] = jnp.sin(ang) * sign_ref[t:t + 1, :]


def _rope_tables(positions):
    t = positions.size
    inv, sign = _rope_patterns()
    tm = TOKEN_TILE
    tab = jax.ShapeDtypeStruct((3, t, LANES), F32)
    return pl.pallas_call(
        _rope_table_kernel,
        grid=(t // tm,),
        in_specs=[pl.BlockSpec((tm, 1), lambda i: (i, 0)),
                  pl.BlockSpec((3, LANES), lambda i: (0, 0)),
                  pl.BlockSpec((3, LANES), lambda i: (0, 0))],
        out_specs=[pl.BlockSpec((3, tm, LANES), lambda i: (0, i, 0)),
                   pl.BlockSpec((3, tm, LANES), lambda i: (0, i, 0))],
        out_shape=[tab, tab],
        compiler_params=_params("parallel"),
        name="rope_tables",
    )(positions.reshape(t, 1), inv, sign)


def _ffn_kernel(x_ref, g_ref, w1_ref, w3_ref, w2_ref, o_ref, h_sc, acc_sc):
    j = pl.program_id(1)

    @pl.when(j == 0)
    def _():
        h_sc[...] = _rms(x_ref[...], g_ref[0:1, :]).astype(BF16)

    h = h_sc[...]
    gate = _dot(h, w1_ref[...])
    up = _dot(h, w3_ref[...])
    act = (gate * jax.nn.sigmoid(gate) * up).astype(BF16)
    part = _dot(act, w2_ref[...])

    @pl.when(j == 0)
    def _():
        acc_sc[...] = part

    @pl.when(j > 0)
    def _():
        acc_sc[...] += part

    @pl.when(j == pl.num_programs(1) - 1)
    def _():
        o_ref[...] = x_ref[...] + 0.5 * _rms(acc_sc[...], g_ref[1:2, :])


def _ffn(x, gains, w13, w2):
    t = x.shape[0]
    tm, tf = TOKEN_TILE, FF_TILE
    nf = D_FF // tf
    return pl.pallas_call(
        _ffn_kernel,
        grid=(t // tm, nf),
        in_specs=[pl.BlockSpec((tm, D_MODEL), lambda i, j: (i, 0)),
                  pl.BlockSpec((2, D_MODEL), lambda i, j: (0, 0)),
                  pl.BlockSpec((D_MODEL, tf), lambda i, j: (0, j)),
                  pl.BlockSpec((D_MODEL, tf), lambda i, j: (0, j + nf)),
                  pl.BlockSpec((tf, D_MODEL), lambda i, j: (j, 0))],
        out_specs=pl.BlockSpec((tm, D_MODEL), lambda i, j: (i, 0)),
        out_shape=jax.ShapeDtypeStruct((t, D_MODEL), F32),
        scratch_shapes=[pltpu.VMEM((tm, D_MODEL), BF16), pltpu.VMEM((tm, D_MODEL), F32)],
        compiler_params=_params("parallel", "arbitrary"),
        name="ffn",
    )(x, gains, w13, w13, w2)


_C_RQ, _C_RK, _C_RV, _C_RG = 0, 256, 512, 1024
_C_MQ, _C_MKV, _C_MKR = 1536, 1792, 1920
_C_DQ, _C_DK, _C_DV = 2048, 2560, 3072
PROJ_WIDTH = 3584


def _inproj_kernel(x_ref, g_ref, wa_ref, qn_ref, kvn_ref, wqb_ref, wkbk_ref, wkbv_ref,
                   cos_ref, sin_ref,
                   rq_ref, rk_ref, rv_ref, rg_ref, mq_ref, mk_ref, mv_ref,
                   dq_ref, dk_ref, dv_ref):
    h = _rms(x_ref[...], g_ref[...]).astype(BF16)

    def proj(start, width):
        return _dot(h, wa_ref[:, start:start + width])

    def rope_store(dst, val, table, half, scale):
        cos, sin = cos_ref[table], sin_ref[table]
        for c in range(val.shape[1] // LANES):
            sl = slice(c * LANES, (c + 1) * LANES)
            r = _rope(val[:, sl], cos, sin, half)
            if scale != 1.0:
                r = r * scale
            dst[:, sl] = r.astype(BF16)

    rope_store(rq_ref, proj(_C_RQ, 256), 0, RET_DK // 2, RET_DK ** -0.5)
    rope_store(rk_ref, proj(_C_RK, 256), 0, RET_DK // 2, 1.0)
    rv_ref[...] = proj(_C_RV, 512).astype(BF16)
    gate = proj(_C_RG, 512)
    rg_ref[...] = (gate * jax.nn.sigmoid(gate)).astype(BF16)

    cq = _rms(proj(_C_MQ, MLA_Q_RANK), qn_ref[...]).astype(BF16)
    q = _dot(cq, wqb_ref[...])
    rope_store(mq_ref, q, 1, MLA_ROPE // 2, (MLA_NOPE + MLA_ROPE) ** -0.5)
    ckv = _rms(proj(_C_MKV, MLA_KV_RANK), kvn_ref[...]).astype(BF16)
    k_rope = _rope(proj(_C_MKR, LANES), cos_ref[1], sin_ref[1], MLA_ROPE // 2)
    k_nope = _dot(ckv, wkbk_ref[...])
    for hd in range(MLA_HEADS):
        sl = slice(hd * LANES, (hd + 1) * LANES)
        mk_ref[:, sl] = (k_nope[:, sl] + k_rope).astype(BF16)
    mv_ref[...] = _dot(ckv, wkbv_ref[...]).astype(BF16)

    rope_store(dq_ref, proj(_C_DQ, 512), 2, DIFF_ROT // 2, DIFF_HD ** -0.5)
    rope_store(dk_ref, proj(_C_DK, 512), 2, DIFF_ROT // 2, 1.0)
    dv_ref[...] = proj(_C_DV, 512).astype(BF16)


def _inproj(x, gain, wa, q_norm, kv_norm, wqb, wkbk, wkbv, cos_tab, sin_tab):
    t = x.shape[0]
    tm = TOKEN_TILE
    widths = (256, 256, 512, 512, 1024, 1024, 512, 512, 512, 512)

    def full(shape):
        return pl.BlockSpec(shape, lambda i: (0,) * len(shape))

    return pl.pallas_call(
        _inproj_kernel,
        grid=(t // tm,),
        in_specs=[pl.BlockSpec((tm, D_MODEL), lambda i: (i, 0)),
                  full((1, D_MODEL)),
                  full((D_MODEL, PROJ_WIDTH)),
                  full((1, MLA_Q_RANK)),
                  full((1, MLA_KV_RANK)),
                  full((MLA_Q_RANK, MLA_HEADS * LANES)),
                  full((MLA_KV_RANK, MLA_HEADS * LANES)),
                  full((MLA_KV_RANK, MLA_HEADS * MLA_DV)),
                  pl.BlockSpec((3, tm, LANES), lambda i: (0, i, 0)),
                  pl.BlockSpec((3, tm, LANES), lambda i: (0, i, 0))],
        out_specs=[pl.BlockSpec((tm, w), lambda i: (i, 0)) for w in widths],
        out_shape=[jax.ShapeDtypeStruct((t, w), BF16) for w in widths],
        compiler_params=_params("parallel"),
        name="inproj",
    )(x, gain, wa, q_norm, kv_norm, wqb, wkbk, wkbv, cos_tab, sin_tab)


def _retention_kernel(q_ref, k_ref, v_ref, g_ref, o_ref, state_sc):
    r = RET_TILE

    @pl.when(pl.program_id(1) == 0)
    def _():
        state_sc[...] = jnp.zeros_like(state_sc)

    row = lax.broadcasted_iota(jnp.int32, (r, r), 0)
    col = lax.broadcasted_iota(jnp.int32, (r, r), 1)
    allowed = (col >> 6) <= (row >> 6)
    dist = jnp.abs(row - col).astype(F32)
    lane = lax.broadcasted_iota(jnp.int32, (r, LANES), 1)
    first_head = lane < RET_DK
    n_local = lax.broadcasted_iota(jnp.int32, (r, LANES), 0).astype(F32)
    s_row = lax.broadcasted_iota(jnp.int32, (LANES, 2 * RET_DV), 0)
    s_col = lax.broadcasted_iota(jnp.int32, (LANES, 2 * RET_DV), 1)
    same_head = (s_row < RET_DK) == (s_col < RET_DV)

    for p in range(RET_HEADS // 2):
        lg0, lg1 = RET_LOG_GAMMA[2 * p], RET_LOG_GAMMA[2 * p + 1]
        q = q_ref[:, p * LANES:(p + 1) * LANES]
        k = k_ref[:, p * LANES:(p + 1) * LANES]
        v = v_ref[:, 2 * p * RET_DV:2 * (p + 1) * RET_DV]
        lg_lane = jnp.where(first_head, lg0, lg1)
        state = state_sc[p]

        q_dec = (q.astype(F32) * jnp.exp(lg_lane * (n_local + 1.0))).astype(BF16)
        o_cross = _dot(q_dec, state.astype(BF16))

        for hh in range(2):
            lg = lg1 if hh else lg0
            qm = jnp.where(first_head != bool(hh), q, jnp.zeros_like(q))
            decay = jnp.where(allowed, jnp.exp(lg * dist), 0.0)
            scores = (_dot_nt(qm, k) * decay).astype(BF16)
            sl = slice(hh * RET_DV, (hh + 1) * RET_DV)
            o = _dot(scores, v[:, sl]) + o_cross[:, sl]
            gsl = slice((2 * p + hh) * RET_DV, (2 * p + hh + 1) * RET_DV)
            o_ref[:, gsl] = (_rms(o) * g_ref[:, gsl].astype(F32)).astype(BF16)

        k_dec = k.astype(F32) * jnp.exp(lg_lane * (r - 1.0 - n_local))
        kv = _dot(k_dec.T.astype(BF16), v)
        block_decay = jnp.where(s_col < RET_DV, math.exp(lg0 * r), math.exp(lg1 * r))
        state_sc[p] = state * block_decay + jnp.where(same_head, kv, 0.0)


def _retention(q, k, v, g):
    b, s, _ = q.shape
    r = RET_TILE

    def spec(w):
        return pl.BlockSpec((None, r, w), lambda i, j: (i, j, 0))

    return pl.pallas_call(
        _retention_kernel,
        grid=(b, s // r),
        in_specs=[spec(256), spec(256), spec(512), spec(512)],
        out_specs=spec(512),
        out_shape=jax.ShapeDtypeStruct((b, s, BRANCH_WIDTH), BF16),
        scratch_shapes=[pltpu.VMEM((RET_HEADS // 2, LANES, 2 * RET_DV), F32)],
        compiler_params=_params("parallel", "arbitrary"),
        name="retention",
    )(q, k, v, g)


def _softmax_parts(q, k_ref, lo, bias):
    tq = q.shape[0]
    s_diag = _dot_nt(q, k_ref[lo:lo + tq, :]) + bias
    m = jnp.max(s_diag, axis=-1, keepdims=True)
    s_past = None
    if lo > 0:
        s_past = _dot_nt(q, k_ref[0:lo, :])
        m = jnp.maximum(m, jnp.max(s_past, axis=-1, keepdims=True))
    p_diag = jnp.exp(s_diag - m)
    total = jnp.sum(p_diag, axis=-1, keepdims=True)
    p_past = None
    if lo > 0:
        p_past = jnp.exp(s_past - m)
        total = total + jnp.sum(p_past, axis=-1, keepdims=True)
    return p_past, p_diag, total


def _mla_attn_kernel(q_ref, k_ref, v_ref, o_ref):
    tq = ATTN_TILE
    s = q_ref.shape[0]
    bias = _chunk_bias(tq)
    lane = lax.broadcasted_iota(jnp.int32, (tq, LANES), 1)
    for i in range(s // tq):
        lo = i * tq
        outs = []
        for hh in range(2):
            cols = slice(hh * LANES, (hh + 1) * LANES)
            p_past, p_diag, total = _softmax_parts(q_ref[lo:lo + tq, cols], k_ref.at[:, cols], lo, bias)
            o = _dot(p_diag.astype(BF16), v_ref[lo:lo + tq, :])
            if p_past is not None:
                o = o + _dot(p_past.astype(BF16), v_ref[0:lo, :])
            outs.append(o / total)
        o_ref[lo:lo + tq, :] = jnp.where(lane < MLA_DV, outs[0], outs[1]).astype(BF16)


def _mla_attention(q, k, v):
    b, s, _ = q.shape
    pairs = MLA_HEADS // 2
    return pl.pallas_call(
        _mla_attn_kernel,
        grid=(b, pairs),
        in_specs=[pl.BlockSpec((None, s, 2 * LANES), lambda i, j: (i, 0, j)),
                  pl.BlockSpec((None, s, 2 * LANES), lambda i, j: (i, 0, j)),
                  pl.BlockSpec((None, s, LANES), lambda i, j: (i, 0, j))],
        out_specs=pl.BlockSpec((None, s, LANES), lambda i, j: (i, 0, j)),
        out_shape=jax.ShapeDtypeStruct((b, s, BRANCH_WIDTH), BF16),
        compiler_params=_params("parallel", "parallel"),
        name="mla_attention",
    )(q, k, v)


def _diff_attn_kernel(q_ref, k_ref, v_ref, lam_ref, linit_ref, o_ref):
    tq = ATTN_TILE
    s = q_ref.shape[0]
    bias = _chunk_bias(tq)
    lane = lax.broadcasted_iota(jnp.int32, (tq, LANES), 1)
    lp = lam_ref[...]
    linit = linit_ref[...]
    lam = (jnp.exp(jnp.sum(lp[0:1] * lp[1:2], axis=-1, keepdims=True))
           - jnp.exp(jnp.sum(lp[2:3] * lp[3:4], axis=-1, keepdims=True)) + linit)
    for i in range(s // tq):
        lo = i * tq
        q = q_ref[lo:lo + tq, :]
        zero = jnp.zeros_like(q)
        w_diag, w_past = None, None
        for j in range(2):
            qj = jnp.where((lane < DIFF_HD) != bool(j), q, zero)
            p_past, p_diag, total = _softmax_parts(qj, k_ref, lo, bias)
            coef = 1.0 / total
            if j:
                coef = -lam * coef
            w_diag = p_diag * coef if w_diag is None else w_diag + p_diag * coef
            if p_past is not None:
                w_past = p_past * coef if w_past is None else w_past + p_past * coef
        o = _dot(w_diag.astype(BF16), v_ref[lo:lo + tq, :])
        if w_past is not None:
            o = o + _dot(w_past.astype(BF16), v_ref[0:lo, :])
        o_ref[lo:lo + tq, :] = (_rms(o) * (1.0 - linit)).astype(BF16)


def _diff_attention(q, k, v, lam_params, lambda_init):
    b, s, _ = q.shape

    def spec():
        return pl.BlockSpec((None, s, LANES), lambda i, j: (i, 0, j))

    return pl.pallas_call(
        _diff_attn_kernel,
        grid=(b, DIFF_HEADS),
        in_specs=[spec(), spec(), spec(),
                  pl.BlockSpec((4, DIFF_HD), lambda i, j: (0, 0)),
                  pl.BlockSpec((1, 1), lambda i, j: (0, 0))],
        out_specs=spec(),
        out_shape=jax.ShapeDtypeStruct((b, s, BRANCH_WIDTH), BF16),
        compiler_params=_params("parallel", "parallel"),
        name="diff_attention",
    )(q, k, v, lam_params, jnp.full((1, 1), lambda_init, F32))


def _merge_kernel(x_ref, g_ref, yr_ref, ym_ref, yd_ref, wg_ref, wb_ref, wo_ref, o_ref):
    x = x_ref[...]
    h = _rms(x, g_ref[0:1, :]).astype(BF16)
    mixed = None
    for n, y_ref in enumerate((yr_ref, ym_ref, yd_ref)):
        gate = jax.nn.sigmoid(_dot(h, wg_ref[:, n * D_MODEL:(n + 1) * D_MODEL]))
        term = gate * _dot(y_ref[...], wb_ref[n])
        mixed = term if mixed is None else mixed + term
    y = _dot(mixed.astype(BF16), wo_ref[...])
    o_ref[...] = x + _rms(y, g_ref[1:2, :])


def _merge(x, gains, y_ret, y_mla, y_diff, wg, wb, wo):
    t = x.shape[0]
    tm = TOKEN_TILE

    def rows(w):
        return pl.BlockSpec((tm, w), lambda i: (i, 0))

    return pl.pallas_call(
        _merge_kernel,
        grid=(t // tm,),
        in_specs=[rows(D_MODEL),
                  pl.BlockSpec((2, D_MODEL), lambda i: (0, 0)),
                  rows(BRANCH_WIDTH), rows(BRANCH_WIDTH), rows(BRANCH_WIDTH),
                  pl.BlockSpec((D_MODEL, N_BRANCH * D_MODEL), lambda i: (0, 0)),
                  pl.BlockSpec((N_BRANCH, BRANCH_WIDTH, D_MODEL), lambda i: (0, 0, 0)),
                  pl.BlockSpec((D_MODEL, D_MODEL), lambda i: (0, 0))],
        out_specs=rows(D_MODEL),
        out_shape=jax.ShapeDtypeStruct((t, D_MODEL), F32),
        compiler_params=_params("parallel"),
        name="merge",
    )(x, gains, y_ret, y_mla, y_diff, wg, wb, wo)


def _memkv_kernel(mem_ref, g_ref, w_ref, o_ref):
    m = _rms(mem_ref[...], g_ref[...]).astype(BF16)
    o_ref[...] = _dot(m, w_ref[...]).astype(BF16)


def _memkv(mem, gain, wkv):
    b, n, _ = mem.shape
    return pl.pallas_call(
        _memkv_kernel,
        grid=(b,),
        in_specs=[pl.BlockSpec((None, n, D_MODEL), lambda i: (i, 0, 0)),
                  pl.BlockSpec((1, D_MODEL), lambda i: (0, 0)),
                  pl.BlockSpec((D_MODEL, 2 * D_MODEL), lambda i: (0, 0))],
        out_specs=pl.BlockSpec((None, n, 2 * D_MODEL), lambda i: (i, 0, 0)),
        out_shape=jax.ShapeDtypeStruct((b, n, 2 * D_MODEL), BF16),
        compiler_params=_params("parallel"),
        name="memkv",
    )(mem, gain, wkv)


def _cross_kernel(x_ref, g_ref, kv_ref, wq_ref, wo_ref, o_ref):
    x = x_ref[...]
    h = _rms(x, g_ref[0:1, :]).astype(BF16)
    q = (_dot(h, wq_ref[...]) * CROSS_HD ** -0.5).astype(BF16)
    heads = []
    for hd in range(CROSS_HEADS):
        cols = slice(hd * CROSS_HD, (hd + 1) * CROSS_HD)
        vcols = slice(D_MODEL + hd * CROSS_HD, D_MODEL + (hd + 1) * CROSS_HD)
        sc = _dot_nt(q[:, cols], kv_ref[:, cols])
        p = jnp.exp(sc - jnp.max(sc, axis=-1, keepdims=True))
        total = jnp.sum(p, axis=-1, keepdims=True)
        heads.append((_dot(p.astype(BF16), kv_ref[:, vcols]) / total).astype(BF16))
    y = _dot(jnp.concatenate(heads, axis=-1), wo_ref[...])
    o_ref[...] = x + _rms(y, g_ref[1:2, :])


def _cross(x, gains, kv, wq, wo):
    b, s, _ = x.shape
    tm = TOKEN_TILE
    n = kv.shape[1]
    return pl.pallas_call(
        _cross_kernel,
        grid=(b, s // tm),
        in_specs=[pl.BlockSpec((None, tm, D_MODEL), lambda i, j: (i, j, 0)),
                  pl.BlockSpec((2, D_MODEL), lambda i, j: (0, 0)),
                  pl.BlockSpec((None, n, 2 * D_MODEL), lambda i, j: (i, 0, 0)),
                  pl.BlockSpec((D_MODEL, D_MODEL), lambda i, j: (0, 0)),
                  pl.BlockSpec((D_MODEL, D_MODEL), lambda i, j: (0, 0))],
        out_specs=pl.BlockSpec((None, tm, D_MODEL), lambda i, j: (i, j, 0)),
        out_shape=jax.ShapeDtypeStruct((b, s, D_MODEL), F32),
        compiler_params=_params("parallel", "parallel"),
        name="cross_attention",
    )(x, gains, kv, wq, wo)


def _pack_inproj(w_in_l, wq_b_l, wkv_b_l):
    d = w_in_l.shape[0]
    k_rope_at = 1920
    wa = jnp.concatenate(
        [w_in_l[:, :k_rope_at],
         jnp.zeros((d, MLA_NOPE), F32),
         w_in_l[:, k_rope_at:k_rope_at + MLA_ROPE],
         jnp.zeros((d, LANES - MLA_NOPE - MLA_ROPE), F32),
         w_in_l[:, k_rope_at + MLA_ROPE:GATE_OFFSET]], axis=1).astype(BF16)
    wg = w_in_l[:, GATE_OFFSET:].astype(BF16)
    wq = wq_b_l.reshape(MLA_Q_RANK, MLA_HEADS, MLA_NOPE + MLA_ROPE)
    wq = jnp.pad(wq, ((0, 0), (0, 0), (0, LANES - MLA_NOPE - MLA_ROPE)))
    wq = wq.reshape(MLA_Q_RANK, MLA_HEADS * LANES).astype(BF16)
    wkv = wkv_b_l.reshape(MLA_KV_RANK, MLA_HEADS, MLA_NOPE + MLA_DV)
    wk = jnp.pad(wkv[:, :, :MLA_NOPE], ((0, 0), (0, 0), (0, LANES - MLA_NOPE)))
    wk = wk.reshape(MLA_KV_RANK, MLA_HEADS * LANES).astype(BF16)
    wv = wkv[:, :, MLA_NOPE:].reshape(MLA_KV_RANK, MLA_HEADS * MLA_DV).astype(BF16)
    return wa, wg, wq, wk, wv


def kernel(x, mem, positions, ffn1_norms, ffn1_w13, ffn1_w2, mix_norms, w_in, mla_q_norm,
           mla_kv_norm, mla_wq_b, mla_wkv_b, diff_lambda, w_branch, w_out, cross_norms,
           cross_wq, cross_wkv, cross_wo, ffn2_norms, ffn2_w13, ffn2_w2):
    b, s, d = x.shape
    t = b * s
    cos_tab, sin_tab = _rope_tables(positions)
    xf = x.reshape(t, d)
    for l in range(DEPTH):
        lambda_init = 0.8 - 0.6 * math.exp(-0.3 * l)
        xf = _ffn(xf, ffn1_norms[l], ffn1_w13[l].astype(BF16), ffn1_w2[l].astype(BF16))

        wa, wg, wqb, wkbk, wkbv = _pack_inproj(w_in[l], mla_wq_b[l], mla_wkv_b[l])
        (rq, rk, rv, rg, mq, mk, mv, dq, dk, dv) = _inproj(
            xf, mix_norms[l, 0:1], wa, mla_q_norm[l:l + 1], mla_kv_norm[l:l + 1],
            wqb, wkbk, wkbv, cos_tab, sin_tab)

        def seq(a):
            return a.reshape(b, s, a.shape[-1])

        y_ret = _retention(seq(rq), seq(rk), seq(rv), seq(rg))
        y_mla = _mla_attention(seq(mq), seq(mk), seq(mv))
        y_diff = _diff_attention(seq(dq), seq(dk), seq(dv), diff_lambda[l], lambda_init)
        xf = _merge(xf, mix_norms[l], y_ret.reshape(t, -1), y_mla.reshape(t, -1),
                    y_diff.reshape(t, -1), wg, w_branch[l].astype(BF16), w_out[l].astype(BF16))

        kv = _memkv(mem, cross_norms[l, 2:3], cross_wkv[l].astype(BF16))
        xf = _cross(xf.reshape(b, s, d), cross_norms[l, 0:2], kv,
                    cross_wq[l].astype(BF16), cross_wo[l].astype(BF16)).reshape(t, d)

        xf = _ffn(xf, ffn2_norms[l], ffn2_w13[l].astype(BF16), ffn2_w2[l].astype(BF16))
    return xf.reshape(b, s, d)
```

```python
import functools
import math

import numpy as np
import jax
import jax.numpy as jnp
from jax import lax
from jax.experimental import pallas as pl
from jax.experimental.pallas import tpu as pltpu

F32 = jnp.float32
BF16 = jnp.bfloat16

D_MODEL = 1024
DEPTH = 4
CHUNK = 64
EPS = 1e-6
NEG_INF = -1e30
ROPE_THETA = 500000.0
RET_THETA = 10000.0

RET_HEADS = 4
RET_DK = 64
RET_DV = 128
MLA_HEADS = 8
MLA_Q_RANK = 256
MLA_KV_RANK = 128
MLA_NOPE = 64
MLA_ROPE = 32
MLA_DV = 64
DIFF_HEADS = 4
DIFF_HD = 64
DIFF_ROT = 16
N_BRANCH = 3
BRANCH_WIDTH = 512
CROSS_HEADS = 4
CROSS_HD = 256
D_FF = 2816
GATE_OFFSET = 3488

LANES = 128
VMEM_LIMIT = 56 * 1024 * 1024

TOKEN_TILE = 512
ATTN_TILE = 256
RET_TILE = 256

RET_LOG_GAMMA = tuple(math.log(1.0 - 2.0 ** (-5.0 - h)) for h in range(RET_HEADS))


def _params(*sem):
    return pltpu.CompilerParams(dimension_semantics=sem, vmem_limit_bytes=VMEM_LIMIT)


def _rms(x, gain=None):
    y = x * lax.rsqrt(jnp.mean(x * x, axis=-1, keepdims=True) + EPS)
    return y if gain is None else y * gain


def _dot(a, b):
    return jnp.dot(a, b, preferred_element_type=F32)


def _dot_nt(a, b):
    return lax.dot_general(a, b, (((1,), (1,)), ((), ())), preferred_element_type=F32)


def _rope(x, cos, sin_signed, half):
    lane = lax.broadcasted_iota(jnp.int32, x.shape, 1)
    first = (lane & (2 * half - 1)) < half
    partner = jnp.where(first, pltpu.roll(x, LANES - half, 1), pltpu.roll(x, half, 1))
    return x * cos + partner * sin_signed


def _chunk_bias(n):
    row = lax.broadcasted_iota(jnp.int32, (n, n), 0)
    col = lax.broadcasted_iota(jnp.int32, (n, n), 1)
    return jnp.where((col >> 6) <= (row >> 6), 0.0, NEG_INF).astype(F32)


def _rope_patterns():
    inv = np.zeros((3, LANES), np.float32)
    sign = np.zeros((3, LANES), np.float32)
    for i in range(LANES):
        j = i % RET_DK
        inv[0, i] = 1.0 / (RET_THETA ** (np.float32(2 * (j % 32)) / RET_DK))
        sign[0, i] = -1.0 if j < 32 else 1.0
        if 64 <= i < 96:
            j = i - 64
            inv[1, i] = 1.0 / (ROPE_THETA ** (np.float32(2 * (j % 16)) / MLA_ROPE))
            sign[1, i] = -1.0 if j < 16 else 1.0
        j = i % DIFF_HD
        if j < DIFF_ROT:
            inv[2, i] = 1.0 / (ROPE_THETA ** (np.float32(2 * (j % 8)) / DIFF_ROT))
            sign[2, i] = -1.0 if j < 8 else 1.0
    return jnp.asarray(inv), jnp.asarray(sign)


def _rope_table_kernel(pos_ref, inv_ref, sign_ref, cos_ref, sin_ref):
    pos = pos_ref[...].astype(F32)
    for t in range(3):
        ang = pos * inv_ref[t:t + 1, :]
        cos_ref[t] = jnp.cos(ang)
        sin_ref[t] = jnp.sin(ang) * sign_ref[t:t + 1, :]


def _rope_tables(positions):
    t = positions.size
    inv, sign = _rope_patterns()
    tm = TOKEN_TILE
    tab = jax.ShapeDtypeStruct((3, t, LANES), F32)
    return pl.pallas_call(
        _rope_table_kernel,
        grid=(t // tm,),
        in_specs=[pl.BlockSpec((tm, 1), lambda i: (i, 0)),
                  pl.BlockSpec((3, LANES), lambda i: (0, 0)),
                  pl.BlockSpec((3, LANES), lambda i: (0, 0))],
        out_specs=[pl.BlockSpec((3, tm, LANES), lambda i: (0, i, 0)),
                   pl.BlockSpec((3, tm, LANES), lambda i: (0, i, 0))],
        out_shape=[tab, tab],
        compiler_params=_params("parallel"),
        name="rope_tables",
    )(positions.reshape(t, 1), inv, sign)


def _ffn_kernel(x_ref, g_ref, w13_ref, w2_ref, o_ref):
    x = x_ref[...]
    h = _rms(x, g_ref[0:1, :]).astype(BF16)
    gate = _dot(h, w13_ref[:, :D_FF])
    up = _dot(h, w13_ref[:, D_FF:])
    act = (gate * jax.nn.sigmoid(gate) * up).astype(BF16)
    y = _dot(act, w2_ref[...])
    o_ref[...] = x + 0.5 * _rms(y, g_ref[1:2, :])


def _resident(shape, layer):
    zeros = (0,) * len(shape)
    return pl.BlockSpec((None,) + tuple(shape), lambda *_: (layer,) + zeros,
                        pipeline_mode=pl.Buffered(1))


def _ffn(x, gains, w13, w2, layer):
    t = x.shape[0]
    tm = TOKEN_TILE
    return pl.pallas_call(
        _ffn_kernel,
        grid=(t // tm,),
        in_specs=[pl.BlockSpec((tm, D_MODEL), lambda i: (i, 0)),
                  _resident((2, D_MODEL), layer),
                  _resident((D_MODEL, 2 * D_FF), layer),
                  _resident((D_FF, D_MODEL), layer)],
        out_specs=pl.BlockSpec((tm, D_MODEL), lambda i: (i, 0)),
        out_shape=jax.ShapeDtypeStruct((t, D_MODEL), F32),
        compiler_params=_params("parallel"),
        name="ffn",
    )(x, gains, w13, w2)


_C_RQ, _C_RK, _C_RV, _C_RG = 0, 256, 512, 1024
_C_MQ, _C_MKV = 1536, 1792
_C_DQ, _C_DK, _C_DV = 2048, 2560, 3072
PROJ_WIDTH = 3584


def _inproj_kernel(x_ref, g_ref, wa_ref, qn_ref, kvn_ref, wqb_ref, wkbk_ref, wkbv_ref,
                   cos_ref, sin_ref,
                   rq_ref, rk_ref, rv_ref, rg_ref, mq_ref, mk_ref, mv_ref,
                   dq_ref, dk_ref, dv_ref):
    h = _rms(x_ref[...], g_ref[0:1, :]).astype(BF16)

    def proj(start, width):
        return _dot(h, wa_ref[:, start:start + width])

    def rope_store(dst, val, table, half, scale):
        cos, sin = cos_ref[table], sin_ref[table]
        for c in range(val.shape[1] // LANES):
            sl = slice(c * LANES, (c + 1) * LANES)
            r = _rope(val[:, sl], cos, sin, half)
            if scale != 1.0:
                r = r * scale
            dst[:, sl] = r.astype(BF16)

    rope_store(rq_ref, proj(_C_RQ, 256), 0, RET_DK // 2, RET_DK ** -0.5)
    rope_store(rk_ref, proj(_C_RK, 256), 0, RET_DK // 2, 1.0)
    rv_ref[...] = proj(_C_RV, 512).astype(BF16)
    gate = proj(_C_RG, 512)
    rg_ref[...] = (gate * jax.nn.sigmoid(gate)).astype(BF16)

    cq = _rms(proj(_C_MQ, MLA_Q_RANK), qn_ref[...]).astype(BF16)
    q = _dot(cq, wqb_ref[...])
    rope_store(mq_ref, q, 1, MLA_ROPE // 2, (MLA_NOPE + MLA_ROPE) ** -0.5)
    kv_a = proj(_C_MKV, MLA_KV_RANK + LANES)
    ckv = _rms(kv_a[:, :MLA_KV_RANK], kvn_ref[...]).astype(BF16)
    k_rope = _rope(kv_a[:, MLA_KV_RANK:], cos_ref[1], sin_ref[1], MLA_ROPE // 2)
    k_nope = _dot(ckv, wkbk_ref[...])
    for hd in range(MLA_HEADS):
        sl = slice(hd * LANES, (hd + 1) * LANES)
        mk_ref[:, sl] = (k_nope[:, sl] + k_rope).astype(BF16)
    mv_ref[...] = _dot(ckv, wkbv_ref[...]).astype(BF16)

    rope_store(dq_ref, proj(_C_DQ, 512), 2, DIFF_ROT // 2, DIFF_HD ** -0.5)
    rope_store(dk_ref, proj(_C_DK, 512), 2, DIFF_ROT // 2, 1.0)
    dv_ref[...] = proj(_C_DV, 512).astype(BF16)


def _inproj(x, gains, wa, q_norm, kv_norm, wqb, wkbk, wkbv, cos_tab, sin_tab, layer):
    t = x.shape[0]
    tm = TOKEN_TILE
    widths = (256, 256, 512, 512, 1024, 1024, 512, 512, 512, 512)

    return pl.pallas_call(
        _inproj_kernel,
        grid=(t // tm,),
        in_specs=[pl.BlockSpec((tm, D_MODEL), lambda i: (i, 0)),
                  _resident((2, D_MODEL), layer),
                  _resident((D_MODEL, PROJ_WIDTH), layer),
                  _resident((1, MLA_Q_RANK), layer),
                  _resident((1, MLA_KV_RANK), layer),
                  _resident((MLA_Q_RANK, MLA_HEADS * LANES), layer),
                  _resident((MLA_KV_RANK, MLA_HEADS * LANES), layer),
                  _resident((MLA_KV_RANK, MLA_HEADS * MLA_DV), layer),
                  pl.BlockSpec((3, tm, LANES), lambda i: (0, i, 0)),
                  pl.BlockSpec((3, tm, LANES), lambda i: (0, i, 0))],
        out_specs=[pl.BlockSpec((tm, w), lambda i: (i, 0)) for w in widths],
        out_shape=[jax.ShapeDtypeStruct((t, w), BF16) for w in widths],
        compiler_params=_params("parallel"),
        name="inproj",
    )(x, gains, wa, q_norm, kv_norm, wqb, wkbk, wkbv, cos_tab, sin_tab)


def _retention_kernel(q_ref, k_ref, v_ref, g_ref, o_ref, state_sc):
    r = RET_TILE

    @pl.when(pl.program_id(1) == 0)
    def _():
        state_sc[...] = jnp.zeros_like(state_sc)

    row = lax.broadcasted_iota(jnp.int32, (r, r), 0)
    col = lax.broadcasted_iota(jnp.int32, (r, r), 1)
    allowed = (col >> 6) <= (row >> 6)
    dist = jnp.abs(row - col).astype(F32)
    lane = lax.broadcasted_iota(jnp.int32, (r, LANES), 1)
    first_head = lane < RET_DK
    n_local = lax.broadcasted_iota(jnp.int32, (r, LANES), 0).astype(F32)
    s_row = lax.broadcasted_iota(jnp.int32, (LANES, 2 * RET_DV), 0)
    s_col = lax.broadcasted_iota(jnp.int32, (LANES, 2 * RET_DV), 1)
    same_head = (s_row < RET_DK) == (s_col < RET_DV)

    for p in range(RET_HEADS // 2):
        lg0, lg1 = RET_LOG_GAMMA[2 * p], RET_LOG_GAMMA[2 * p + 1]
        q = q_ref[:, p * LANES:(p + 1) * LANES]
        k = k_ref[:, p * LANES:(p + 1) * LANES]
        v = v_ref[:, 2 * p * RET_DV:2 * (p + 1) * RET_DV]
        lg_lane = jnp.where(first_head, lg0, lg1)
        state = state_sc[p]

        q_dec = (q.astype(F32) * jnp.exp(lg_lane * (n_local + 1.0))).astype(BF16)
        o_cross = _dot(q_dec, state.astype(BF16))

        for hh in range(2):
            lg = lg1 if hh else lg0
            qm = jnp.where(first_head != bool(hh), q, jnp.zeros_like(q))
            decay = jnp.where(allowed, jnp.exp(lg * dist), 0.0)
            scores = (_dot_nt(qm, k) * decay).astype(BF16)
            sl = slice(hh * RET_DV, (hh + 1) * RET_DV)
            o = _dot(scores, v[:, sl]) + o_cross[:, sl]
            gsl = slice((2 * p + hh) * RET_DV, (2 * p + hh + 1) * RET_DV)
            o_ref[:, gsl] = (_rms(o) * g_ref[:, gsl].astype(F32)).astype(BF16)

        k_dec = k.astype(F32) * jnp.exp(lg_lane * (r - 1.0 - n_local))
        kv = _dot(k_dec.T.astype(BF16), v)
        block_decay = jnp.where(s_col < RET_DV, math.exp(lg0 * r), math.exp(lg1 * r))
        state_sc[p] = state * block_decay + jnp.where(same_head, kv, 0.0)


def _retention(q, k, v, g):
    b, s, _ = q.shape
    r = RET_TILE

    def spec(w):
        return pl.BlockSpec((None, r, w), lambda i, j: (i, j, 0))

    return pl.pallas_call(
        _retention_kernel,
        grid=(b, s // r),
        in_specs=[spec(256), spec(256), spec(512), spec(512)],
        out_specs=spec(512),
        out_shape=jax.ShapeDtypeStruct((b, s, BRANCH_WIDTH), BF16),
        scratch_shapes=[pltpu.VMEM((RET_HEADS // 2, LANES, 2 * RET_DV), F32)],
        compiler_params=_params("parallel", "arbitrary"),
        name="retention",
    )(q, k, v, g)


def _softmax_parts(q, k_ref, lo, bias):
    tq = q.shape[0]
    s_diag = _dot_nt(q, k_ref[lo:lo + tq, :]) + bias
    m = jnp.max(s_diag, axis=-1, keepdims=True)
    s_past = None
    if lo > 0:
        s_past = _dot_nt(q, k_ref[0:lo, :])
        m = jnp.maximum(m, jnp.max(s_past, axis=-1, keepdims=True))
    p_diag = jnp.exp(s_diag - m)
    total = jnp.sum(p_diag, axis=-1, keepdims=True)
    p_past = None
    if lo > 0:
        p_past = jnp.exp(s_past - m)
        total = total + jnp.sum(p_past, axis=-1, keepdims=True)
    return p_past, p_diag, total


def _mla_attn_kernel(q_ref, k_ref, v_ref, o_ref):
    tq = ATTN_TILE
    s = q_ref.shape[0]
    bias = _chunk_bias(tq)
    lane = lax.broadcasted_iota(jnp.int32, (tq, LANES), 1)
    for i in range(s // tq):
        lo = i * tq
        outs = []
        for hh in range(2):
            cols = slice(hh * LANES, (hh + 1) * LANES)
            p_past, p_diag, total = _softmax_parts(q_ref[lo:lo + tq, cols], k_ref.at[:, cols], lo, bias)
            o = _dot(p_diag.astype(BF16), v_ref[lo:lo + tq, :])
            if p_past is not None:
                o = o + _dot(p_past.astype(BF16), v_ref[0:lo, :])
            outs.append(o / total)
        o_ref[lo:lo + tq, :] = jnp.where(lane < MLA_DV, outs[0], outs[1]).astype(BF16)


def _mla_attention(q, k, v):
    b, s, _ = q.shape
    pairs = MLA_HEADS // 2
    return pl.pallas_call(
        _mla_attn_kernel,
        grid=(b, pairs),
        in_specs=[pl.BlockSpec((None, s, 2 * LANES), lambda i, j: (i, 0, j)),
                  pl.BlockSpec((None, s, 2 * LANES), lambda i, j: (i, 0, j)),
                  pl.BlockSpec((None, s, LANES), lambda i, j: (i, 0, j))],
        out_specs=pl.BlockSpec((None, s, LANES), lambda i, j: (i, 0, j)),
        out_shape=jax.ShapeDtypeStruct((b, s, BRANCH_WIDTH), BF16),
        compiler_params=_params("parallel", "parallel"),
        name="mla_attention",
    )(q, k, v)


def _diff_attn_kernel(q_ref, k_ref, v_ref, lam_ref, linit_ref, o_ref):
    tq = ATTN_TILE
    s = q_ref.shape[0]
    bias = _chunk_bias(tq)
    lane = lax.broadcasted_iota(jnp.int32, (tq, LANES), 1)
    lp = lam_ref[...]
    linit = linit_ref[...]
    lam = (jnp.exp(jnp.sum(lp[0:1] * lp[1:2], axis=-1, keepdims=True))
           - jnp.exp(jnp.sum(lp[2:3] * lp[3:4], axis=-1, keepdims=True)) + linit)
    for i in range(s // tq):
        lo = i * tq
        q = q_ref[lo:lo + tq, :]
        zero = jnp.zeros_like(q)
        w_diag, w_past = None, None
        for j in range(2):
            qj = jnp.where((lane < DIFF_HD) != bool(j), q, zero)
            p_past, p_diag, total = _softmax_parts(qj, k_ref, lo, bias)
            coef = 1.0 / total
            if j:
                coef = -lam * coef
            w_diag = p_diag * coef if w_diag is None else w_diag + p_diag * coef
            if p_past is not None:
                w_past = p_past * coef if w_past is None else w_past + p_past * coef
        o = _dot(w_diag.astype(BF16), v_ref[lo:lo + tq, :])
        if w_past is not None:
            o = o + _dot(w_past.astype(BF16), v_ref[0:lo, :])
        o_ref[lo:lo + tq, :] = (_rms(o) * (1.0 - linit)).astype(BF16)


def _diff_attention(q, k, v, lam_params, lambda_init):
    b, s, _ = q.shape

    def spec():
        return pl.BlockSpec((None, s, LANES), lambda i, j: (i, 0, j))

    return pl.pallas_call(
        _diff_attn_kernel,
        grid=(b, DIFF_HEADS),
        in_specs=[spec(), spec(), spec(),
                  pl.BlockSpec((4, DIFF_HD), lambda i, j: (0, 0)),
                  pl.BlockSpec((1, 1), lambda i, j: (0, 0))],
        out_specs=spec(),
        out_shape=jax.ShapeDtypeStruct((b, s, BRANCH_WIDTH), BF16),
        compiler_params=_params("parallel", "parallel"),
        name="diff_attention",
    )(q, k, v, lam_params, jnp.full((1, 1), lambda_init, F32))


def _merge_kernel(x_ref, g_ref, yr_ref, ym_ref, yd_ref, wg_ref, wb_ref, wo_ref, o_ref):
    x = x_ref[...]
    h = _rms(x, g_ref[0:1, :]).astype(BF16)
    mixed = None
    for n, y_ref in enumerate((yr_ref, ym_ref, yd_ref)):
        gate = jax.nn.sigmoid(_dot(h, wg_ref[:, n * D_MODEL:(n + 1) * D_MODEL]))
        term = gate * _dot(y_ref[...], wb_ref[n])
        mixed = term if mixed is None else mixed + term
    y = _dot(mixed.astype(BF16), wo_ref[...])
    o_ref[...] = x + _rms(y, g_ref[1:2, :])


def _merge(x, gains, y_ret, y_mla, y_diff, wg, wb, wo, layer):
    t = x.shape[0]
    tm = TOKEN_TILE

    def rows(w):
        return pl.BlockSpec((tm, w), lambda i: (i, 0))

    return pl.pallas_call(
        _merge_kernel,
        grid=(t // tm,),
        in_specs=[rows(D_MODEL),
                  _resident((2, D_MODEL), layer),
                  rows(BRANCH_WIDTH), rows(BRANCH_WIDTH), rows(BRANCH_WIDTH),
                  _resident((D_MODEL, N_BRANCH * D_MODEL), layer),
                  _resident((N_BRANCH, BRANCH_WIDTH, D_MODEL), layer),
                  _resident((D_MODEL, D_MODEL), layer)],
        out_specs=rows(D_MODEL),
        out_shape=jax.ShapeDtypeStruct((t, D_MODEL), F32),
        compiler_params=_params("parallel"),
        name="merge",
    )(x, gains, y_ret, y_mla, y_diff, wg, wb, wo)


def _memkv_kernel(mem_ref, g_ref, w_ref, o_ref):
    m = _rms(mem_ref[...], g_ref[2:3, :]).astype(BF16)
    o_ref[...] = _dot(m, w_ref[...]).astype(BF16)


def _memkv(mem, gains, wkv):
    b, n, _ = mem.shape
    depth = wkv.shape[0]
    return pl.pallas_call(
        _memkv_kernel,
        grid=(depth, b),
        in_specs=[pl.BlockSpec((None, n, D_MODEL), lambda l, i: (i, 0, 0)),
                  pl.BlockSpec((None, 3, D_MODEL), lambda l, i: (l, 0, 0)),
                  pl.BlockSpec((None, D_MODEL, 2 * D_MODEL), lambda l, i: (l, 0, 0))],
        out_specs=pl.BlockSpec((None, None, n, 2 * D_MODEL), lambda l, i: (l, i, 0, 0)),
        out_shape=jax.ShapeDtypeStruct((depth, b, n, 2 * D_MODEL), BF16),
        compiler_params=_params("parallel", "parallel"),
        name="memkv",
    )(mem, gains, wkv)


def _cross_kernel(x_ref, g_ref, kv_ref, wq_ref, wo_ref, o_ref):
    x = x_ref[...]
    h = _rms(x, g_ref[0:1, :]).astype(BF16)
    q = (_dot(h, wq_ref[...]) * CROSS_HD ** -0.5).astype(BF16)
    heads = []
    for hd in range(CROSS_HEADS):
        cols = slice(hd * CROSS_HD, (hd + 1) * CROSS_HD)
        vcols = slice(D_MODEL + hd * CROSS_HD, D_MODEL + (hd + 1) * CROSS_HD)
        sc = _dot_nt(q[:, cols], kv_ref[:, cols])
        p = jnp.exp(sc - jnp.max(sc, axis=-1, keepdims=True))
        total = jnp.sum(p, axis=-1, keepdims=True)
        heads.append((_dot(p.astype(BF16), kv_ref[:, vcols]) / total).astype(BF16))
    y = _dot(jnp.concatenate(heads, axis=-1), wo_ref[...])
    o_ref[...] = x + _rms(y, g_ref[1:2, :])


def _cross(x, gains, kv, wq, wo, layer):
    b, s, _ = x.shape
    tm = TOKEN_TILE
    n = kv.shape[2]
    return pl.pallas_call(
        _cross_kernel,
        grid=(b, s // tm),
        in_specs=[pl.BlockSpec((None, tm, D_MODEL), lambda i, j: (i, j, 0)),
                  _resident((3, D_MODEL), layer),
                  pl.BlockSpec((None, None, n, 2 * D_MODEL), lambda i, j: (layer, i, 0, 0)),
                  _resident((D_MODEL, D_MODEL), layer),
                  _resident((D_MODEL, D_MODEL), layer)],
        out_specs=pl.BlockSpec((None, tm, D_MODEL), lambda i, j: (i, j, 0)),
        out_shape=jax.ShapeDtypeStruct((b, s, D_MODEL), F32),
        compiler_params=_params("parallel", "parallel"),
        name="cross_attention",
    )(x, gains, kv, wq, wo)


def _pack_inproj(w_in, wq_b, wkv_b):
    depth, d, _ = w_in.shape
    k_rope_at = 1920
    wa = jnp.concatenate(
        [w_in[:, :, :k_rope_at].astype(BF16),
         jnp.zeros((depth, d, MLA_NOPE), BF16),
         w_in[:, :, k_rope_at:k_rope_at + MLA_ROPE].astype(BF16),
         jnp.zeros((depth, d, LANES - MLA_NOPE - MLA_ROPE), BF16),
         w_in[:, :, k_rope_at + MLA_ROPE:GATE_OFFSET].astype(BF16)], axis=2)
    wg = w_in[:, :, GATE_OFFSET:].astype(BF16)
    wq = wq_b.astype(BF16).reshape(depth, MLA_Q_RANK, MLA_HEADS, MLA_NOPE + MLA_ROPE)
    wq = jnp.pad(wq, ((0, 0), (0, 0), (0, 0), (0, LANES - MLA_NOPE - MLA_ROPE)))
    wq = wq.reshape(depth, MLA_Q_RANK, MLA_HEADS * LANES)
    wkv = wkv_b.astype(BF16).reshape(depth, MLA_KV_RANK, MLA_HEADS, MLA_NOPE + MLA_DV)
    wk = jnp.pad(wkv[..., :MLA_NOPE], ((0, 0), (0, 0), (0, 0), (0, LANES - MLA_NOPE)))
    wk = wk.reshape(depth, MLA_KV_RANK, MLA_HEADS * LANES)
    wv = wkv[..., MLA_NOPE:].reshape(depth, MLA_KV_RANK, MLA_HEADS * MLA_DV)
    return wa, wg, wq, wk, wv


def kernel(x, mem, positions, ffn1_norms, ffn1_w13, ffn1_w2, mix_norms, w_in, mla_q_norm,
           mla_kv_norm, mla_wq_b, mla_wkv_b, diff_lambda, w_branch, w_out, cross_norms,
           cross_wq, cross_wkv, cross_wo, ffn2_norms, ffn2_w13, ffn2_w2):
    b, s, d = x.shape
    t = b * s
    depth = w_in.shape[0]
    cos_tab, sin_tab = _rope_tables(positions)

    f1_w13, f1_w2 = ffn1_w13.astype(BF16), ffn1_w2.astype(BF16)
    f2_w13, f2_w2 = ffn2_w13.astype(BF16), ffn2_w2.astype(BF16)
    wa, wg, wqb, wkbk, wkbv = _pack_inproj(w_in, mla_wq_b, mla_wkv_b)
    wb, wo = w_branch.astype(BF16), w_out.astype(BF16)
    c_wq, c_wo = cross_wq.astype(BF16), cross_wo.astype(BF16)
    q_norm = mla_q_norm.reshape(depth, 1, MLA_Q_RANK)
    kv_norm = mla_kv_norm.reshape(depth, 1, MLA_KV_RANK)

    mem_kv = _memkv(mem, cross_norms, cross_wkv.astype(BF16))

    def seq(a):
        return a.reshape(b, s, a.shape[-1])

    xf = x.reshape(t, d)
    for l in range(depth):
        lambda_init = 0.8 - 0.6 * math.exp(-0.3 * l)
        xf = _ffn(xf, ffn1_norms, f1_w13, f1_w2, l)

        (rq, rk, rv, rg, mq, mk, mv, dq, dk, dv) = _inproj(
            xf, mix_norms, wa, q_norm, kv_norm, wqb, wkbk, wkbv, cos_tab, sin_tab, l)
        y_ret = _retention(seq(rq), seq(rk), seq(rv), seq(rg))
        y_mla = _mla_attention(seq(mq), seq(mk), seq(mv))
        y_diff = _diff_attention(seq(dq), seq(dk), seq(dv), diff_lambda[l], lambda_init)
        xf = _merge(xf, mix_norms, y_ret.reshape(t, -1), y_mla.reshape(t, -1),
                    y_diff.reshape(t, -1), wg, wb, wo, l)

        xf = _cross(xf.reshape(b, s, d), cross_norms, mem_kv, c_wq, c_wo, l).reshape(t, d)

        xf = _ffn(xf, ffn2_norms, f2_w13, f2_w2, l)
    return xf.reshape(b, s, d)
```

```python
import functools
import math

import numpy as np
import jax
import jax.numpy as jnp
from jax import lax
from jax.experimental import pallas as pl
from jax.experimental.pallas import tpu as pltpu

F32 = jnp.float32
BF16 = jnp.bfloat16

D_MODEL = 1024
DEPTH = 4
CHUNK = 64
EPS = 1e-6
NEG_INF = -1e30
ROPE_THETA = 500000.0
RET_THETA = 10000.0

RET_HEADS = 4
RET_DK = 64
RET_DV = 128
MLA_HEADS = 8
MLA_Q_RANK = 256
MLA_KV_RANK = 128
MLA_NOPE = 64
MLA_ROPE = 32
MLA_DV = 64
DIFF_HEADS = 4
DIFF_HD = 64
DIFF_ROT = 16
N_BRANCH = 3
BRANCH_WIDTH = 512
CROSS_HEADS = 4
CROSS_HD = 256
D_FF = 2816
GATE_OFFSET = 3488

LANES = 128
VMEM_LIMIT = 56 * 1024 * 1024

TOKEN_TILE = 512
ATTN_TILE = 256
RET_TILE = 256

LOG2E = math.log2(math.e)
RET_LOG_GAMMA =tuple(math.log(1.0 - 2.0 ** (-5.0 - h)) for h in range(RET_HEADS))


def _params(*sem):
    return pltpu.CompilerParams(dimension_semantics=sem, vmem_limit_bytes=VMEM_LIMIT)


def _rms(x, gain=None):
    y = x * lax.rsqrt(jnp.mean(x * x, axis=-1, keepdims=True) + EPS)
    return y if gain is None else y * gain


def _dot(a, b):
    return jnp.dot(a, b, preferred_element_type=F32)


def _dot_nt(a, b):
    return lax.dot_general(a, b, (((1,), (1,)), ((), ())), preferred_element_type=F32)


def _rope(x, cos, sin_signed, half):
    lane = lax.broadcasted_iota(jnp.int32, x.shape, 1)
    first = (lane & (2 * half - 1)) < half
    partner = jnp.where(first, pltpu.roll(x, LANES - half, 1), pltpu.roll(x, half, 1))
    return x * cos + partner * sin_signed


def _rope_patterns():
    inv = np.zeros((3, LANES), np.float32)
    sign = np.zeros((3, LANES), np.float32)
    for i in range(LANES):
        j = i % RET_DK
        inv[0, i] = 1.0 / (RET_THETA ** (np.float32(2 * (j % 32)) / RET_DK))
        sign[0, i] = -1.0 if j < 32 else 1.0
        if 64 <= i < 96:
            j = i - 64
            inv[1, i] = 1.0 / (ROPE_THETA ** (np.float32(2 * (j % 16)) / MLA_ROPE))
            sign[1, i] = -1.0 if j < 16 else 1.0
        j = i % DIFF_HD
        if j < DIFF_ROT:
            inv[2, i] = 1.0 / (ROPE_THETA ** (np.float32(2 * (j % 8)) / DIFF_ROT))
            sign[2, i] = -1.0 if j < 8 else 1.0
    return jnp.asarray(inv), jnp.asarray(sign)


def _rope_table_kernel(pos_ref, inv_ref, sign_ref, cos_ref, sin_ref):
    pos = pos_ref[...].astype(F32)
    for t in range(3):
        ang = pos * inv_ref[t:t + 1, :]
        cos_ref[t] = jnp.cos(ang)
        sin_ref[t] = jnp.sin(ang) * sign_ref[t:t + 1, :]


def _rope_tables(positions):
    t = positions.size
    inv, sign = _rope_patterns()
    tm = TOKEN_TILE
    tab = jax.ShapeDtypeStruct((3, t, LANES), F32)
    return pl.pallas_call(
        _rope_table_kernel,
        grid=(t // tm,),
        in_specs=[pl.BlockSpec((tm, 1), lambda i: (i, 0)),
                  pl.BlockSpec((3, LANES), lambda i: (0, 0)),
                  pl.BlockSpec((3, LANES), lambda i: (0, 0))],
        out_specs=[pl.BlockSpec((3, tm, LANES), lambda i: (0, i, 0)),
                   pl.BlockSpec((3, tm, LANES), lambda i: (0, i, 0))],
        out_shape=[tab, tab],
        compiler_params=_params("parallel"),
        name="rope_tables",
    )(positions.reshape(t, 1), inv, sign)


def _ffn_kernel(x_ref, g_ref, w13_ref, w2_ref, o_ref):
    x = x_ref[...]
    h = _rms(x, g_ref[0:1, :]).astype(BF16)
    gate = _dot(h, w13_ref[:, :D_FF])
    up = _dot(h, w13_ref[:, D_FF:])
    act = (gate * jax.nn.sigmoid(gate) * up).astype(BF16)
    y = _dot(act, w2_ref[...])
    o_ref[...] = x + 0.5 * _rms(y, g_ref[1:2, :])


def _resident(shape, layer):
    zeros = (0,) * len(shape)
    return pl.BlockSpec((None,) + tuple(shape), lambda *_: (layer,) + zeros,
                        pipeline_mode=pl.Buffered(1))


def _ffn(x, gains, w13, w2, layer):
    t = x.shape[0]
    tm = TOKEN_TILE
    return pl.pallas_call(
        _ffn_kernel,
        grid=(t // tm,),
        in_specs=[pl.BlockSpec((tm, D_MODEL), lambda i: (i, 0)),
                  _resident((2, D_MODEL), layer),
                  _resident((D_MODEL, 2 * D_FF), layer),
                  _resident((D_FF, D_MODEL), layer)],
        out_specs=pl.BlockSpec((tm, D_MODEL), lambda i: (i, 0)),
        out_shape=jax.ShapeDtypeStruct((t, D_MODEL), F32),
        compiler_params=_params("parallel"),
        name="ffn",
    )(x, gains, w13, w2)


_C_RQ, _C_RK, _C_RV, _C_RG = 0, 256, 512, 1024
_C_MQ, _C_MKV = 1536, 1792
_C_DQ, _C_DK, _C_DV = 2048, 2560, 3072
PROJ_WIDTH = 3584


def _inproj_kernel(x_ref, g_ref, wa_ref, qn_ref, kvn_ref, wqb_ref, wkbk_ref, wkbv_ref,
                   cos_ref, sin_ref,
                   rq_ref, rk_ref, rv_ref, rg_ref, mq_ref, mk_ref, mv_ref,
                   dq_ref, dk_ref, dv_ref):
    h = _rms(x_ref[...], g_ref[0:1, :]).astype(BF16)

    def proj(start, width):
        return _dot(h, wa_ref[:, start:start + width])

    def rope_store(dst, val, table, half, scale):
        cos, sin = cos_ref[table], sin_ref[table]
        for c in range(val.shape[1] // LANES):
            sl = slice(c * LANES, (c + 1) * LANES)
            r = _rope(val[:, sl], cos, sin, half)
            if scale != 1.0:
                r = r * scale
            dst[:, sl] = r.astype(BF16)

    rope_store(rq_ref, proj(_C_RQ, 256), 0, RET_DK // 2, RET_DK ** -0.5)
    rope_store(rk_ref, proj(_C_RK, 256), 0, RET_DK // 2, 1.0)
    rv_ref[...] = proj(_C_RV, 512).astype(BF16)
    gate = proj(_C_RG, 512)
    rg_ref[...] = (gate * jax.nn.sigmoid(gate)).astype(BF16)

    cq = _rms(proj(_C_MQ, MLA_Q_RANK), qn_ref[...]).astype(BF16)
    q = _dot(cq, wqb_ref[...])
    rope_store(mq_ref, q, 1, MLA_ROPE // 2, LOG2E * (MLA_NOPE + MLA_ROPE) ** -0.5)
    kv_a = proj(_C_MKV, MLA_KV_RANK + LANES)
    ckv = _rms(kv_a[:, :MLA_KV_RANK], kvn_ref[...]).astype(BF16)
    k_rope = _rope(kv_a[:, MLA_KV_RANK:], cos_ref[1], sin_ref[1], MLA_ROPE // 2)
    k_nope = _dot(ckv, wkbk_ref[...])
    for hd in range(MLA_HEADS):
        sl = slice(hd * LANES, (hd + 1) * LANES)
        mk_ref[:, sl] = (k_nope[:, sl] + k_rope).astype(BF16)
    mv_ref[...] = _dot(ckv, wkbv_ref[...]).astype(BF16)

    rope_store(dq_ref, proj(_C_DQ, 512), 2, DIFF_ROT // 2, LOG2E * DIFF_HD ** -0.5)
    rope_store(dk_ref, proj(_C_DK, 512), 2, DIFF_ROT // 2, 1.0)
    dv_ref[...] = proj(_C_DV, 512).astype(BF16)


def _inproj(x, gains, wa, q_norm, kv_norm, wqb, wkbk, wkbv, cos_tab, sin_tab, layer):
    t = x.shape[0]
    tm = TOKEN_TILE
    widths = (256, 256, 512, 512, 1024, 1024, 512, 512, 512, 512)

    return pl.pallas_call(
        _inproj_kernel,
        grid=(t // tm,),
        in_specs=[pl.BlockSpec((tm, D_MODEL), lambda i: (i, 0)),
                  _resident((2, D_MODEL), layer),
                  _resident((D_MODEL, PROJ_WIDTH), layer),
                  _resident((1, MLA_Q_RANK), layer),
                  _resident((1, MLA_KV_RANK), layer),
                  _resident((MLA_Q_RANK, MLA_HEADS * LANES), layer),
                  _resident((MLA_KV_RANK, MLA_HEADS * LANES), layer),
                  _resident((MLA_KV_RANK, MLA_HEADS * MLA_DV), layer),
                  pl.BlockSpec((3, tm, LANES), lambda i: (0, i, 0)),
                  pl.BlockSpec((3, tm, LANES), lambda i: (0, i, 0))],
        out_specs=[pl.BlockSpec((tm, w), lambda i: (i, 0)) for w in widths],
        out_shape=[jax.ShapeDtypeStruct((t, w), BF16) for w in widths],
        compiler_params=_params("parallel"),
        name="inproj",
    )(x, gains, wa, q_norm, kv_norm, wqb, wkbk, wkbv, cos_tab, sin_tab)


def _retention_kernel(q_ref, k_ref, v_ref, g_ref, o_ref, state_sc):
    r = RET_TILE

    @pl.when(pl.program_id(1) == 0)
    def _():
        state_sc[...] = jnp.zeros_like(state_sc)

    row = lax.broadcasted_iota(jnp.int32, (r, r), 0)
    col = lax.broadcasted_iota(jnp.int32, (r, r), 1)
    allowed = (col >> 6) <= (row >> 6)
    dist = jnp.abs(row - col).astype(F32)
    lane = lax.broadcasted_iota(jnp.int32, (r, LANES), 1)
    first_head = lane < RET_DK
    n_local = lax.broadcasted_iota(jnp.int32, (r, LANES), 0).astype(F32)
    s_row = lax.broadcasted_iota(jnp.int32, (LANES, 2 * RET_DV), 0)
    s_col = lax.broadcasted_iota(jnp.int32, (LANES, 2 * RET_DV), 1)
    same_head = (s_row < RET_DK) == (s_col < RET_DV)

    for p in range(RET_HEADS // 2):
        lg0, lg1 = RET_LOG_GAMMA[2 * p], RET_LOG_GAMMA[2 * p + 1]
        q = q_ref[:, p * LANES:(p + 1) * LANES]
        k = k_ref[:, p * LANES:(p + 1) * LANES]
        v = v_ref[:, 2 * p * RET_DV:2 * (p + 1) * RET_DV]
        lg_lane = jnp.where(first_head, lg0, lg1)
        state = state_sc[p]

        q_dec = (q.astype(F32) * jnp.exp(lg_lane * (n_local + 1.0))).astype(BF16)
        o_cross = _dot(q_dec, state.astype(BF16))

        for hh in range(2):
            lg = lg1 if hh else lg0
            qm = jnp.where(first_head != bool(hh), q, jnp.zeros_like(q))
            decay = jnp.where(allowed, jnp.exp(lg * dist), 0.0)
            scores = (_dot_nt(qm, k) * decay).astype(BF16)
            sl = slice(hh * RET_DV, (hh + 1) * RET_DV)
            o = _dot(scores, v[:, sl]) + o_cross[:, sl]
            gsl = slice((2 * p + hh) * RET_DV, (2 * p + hh + 1) * RET_DV)
            o_ref[:, gsl] = (_rms(o) * g_ref[:, gsl].astype(F32)).astype(BF16)

        k_dec = k.astype(F32) * jnp.exp(lg_lane * (r - 1.0 - n_local))
        kv = _dot(k_dec.T.astype(BF16), v)
        block_decay = jnp.where(s_col < RET_DV, math.exp(lg0 * r), math.exp(lg1 * r))
        state_sc[p] = state * block_decay + jnp.where(same_head, kv, 0.0)


def _retention(q, k, v, g):
    b, s, _ = q.shape
    r = RET_TILE

    def spec(w):
        return pl.BlockSpec((None, r, w), lambda i, j: (i, j, 0))

    return pl.pallas_call(
        _retention_kernel,
        grid=(b, s // r),
        in_specs=[spec(256), spec(256), spec(512), spec(512)],
        out_specs=spec(512),
        out_shape=jax.ShapeDtypeStruct((b, s, BRANCH_WIDTH), BF16),
        scratch_shapes=[pltpu.VMEM((RET_HEADS // 2, LANES, 2 * RET_DV), F32)],
        compiler_params=_params("parallel", "arbitrary"),
        name="retention",
    )(q, k, v, g)


ONES_ROWS = 16


def _scores_t(q, k_ref, lo, bias_t):
    tq = q.shape[0]
    s_diag = _dot_nt(k_ref[lo:lo + tq, :], q) + bias_t
    m = jnp.max(s_diag, axis=0, keepdims=True)
    s_past = None
    if lo > 0:
        s_past = _dot_nt(k_ref[0:lo, :], q)
        m = jnp.maximum(m, jnp.max(s_past, axis=0, keepdims=True))
    return s_diag, s_past, m


def _weighted_values_t(scores, vt_ref, lo):
    s_diag, s_past, m = scores
    tq = s_diag.shape[1]
    out = _dot(vt_ref[:, lo:lo + tq], jnp.exp2(s_diag - m).astype(BF16))
    if s_past is not None:
        out = out + _dot(vt_ref[:, 0:lo], jnp.exp2(s_past - m).astype(BF16))
    return out


def _skewed(n_items, first_stage, second_stage):
    pending = first_stage(0)
    for n in range(n_items):
        ahead = first_stage(n + 1) if n + 1 < n_items else None
        second_stage(n, pending)
        pending = ahead


def _chunk_bias_t(n):
    key = lax.broadcasted_iota(jnp.int32, (n, n), 0)
    qry = lax.broadcasted_iota(jnp.int32, (n, n), 1)
    return jnp.where((key >> 6) <= (qry >> 6), 0.0, NEG_INF).astype(F32)


def _mla_attn_kernel(q_ref, k_ref, v_ref, o_ref, vt_sc):
    tq = ATTN_TILE
    s = q_ref.shape[0]
    vt = v_ref[...].astype(F32).T
    for hh in range(2):
        vt_sc[hh, 0:MLA_DV, :] = vt[hh * MLA_DV:(hh + 1) * MLA_DV, :].astype(BF16)
        vt_sc[hh, MLA_DV:, :] = jnp.ones((ONES_ROWS, s), BF16)
    bias_t = _chunk_bias_t(tq)

    def scores(i):
        lo = i * tq
        return [_scores_t(q_ref[lo:lo + tq, hh * LANES:(hh + 1) * LANES],
                          k_ref.at[:, hh * LANES:(hh + 1) * LANES], lo, bias_t) for hh in range(2)]

    def outputs(i, sc):
        lo = i * tq
        halves = []
        for hh in range(2):
            acc = _weighted_values_t(sc[hh], vt_sc.at[hh], lo)
            halves.append(acc[0:MLA_DV, :] / acc[MLA_DV:MLA_DV + 1, :])
        o_ref[lo:lo + tq, :] = jnp.concatenate(halves, axis=0).T.astype(BF16)

    _skewed(s // tq, scores, outputs)


def _mla_attention(q, k, v):
    b, s, _ = q.shape
    pairs = MLA_HEADS // 2
    return pl.pallas_call(
        _mla_attn_kernel,
        grid=(b, pairs),
        in_specs=[pl.BlockSpec((None, s, 2 * LANES), lambda i, j: (i, 0, j)),
                  pl.BlockSpec((None, s, 2 * LANES), lambda i, j: (i, 0, j)),
                  pl.BlockSpec((None, s, LANES), lambda i, j: (i, 0, j))],
        out_specs=pl.BlockSpec((None, s, LANES), lambda i, j: (i, 0, j)),
        out_shape=jax.ShapeDtypeStruct((b, s, BRANCH_WIDTH), BF16),
        scratch_shapes=[pltpu.VMEM((2, MLA_DV + ONES_ROWS, s), BF16)],
        compiler_params=_params("parallel", "parallel"),
        name="mla_attention",
    )(q, k, v)


def _diff_attn_kernel(q_ref, k_ref, v_ref, lam_ref, linit_ref, o_ref, vt_sc):
    tq = ATTN_TILE
    s = q_ref.shape[0]
    dv = 2 * DIFF_HD
    vt_sc[0:dv, :] = v_ref[...].astype(F32).T.astype(BF16)
    vt_sc[dv:, :] = jnp.ones((ONES_ROWS, s), BF16)
    bias_t = _chunk_bias_t(tq)
    lane = lax.broadcasted_iota(jnp.int32, (tq, LANES), 1)
    lp = lam_ref[...]
    linit = linit_ref[...]
    lam = (jnp.exp(jnp.sum(lp[0:1] * lp[1:2], axis=-1, keepdims=True))
           - jnp.exp(jnp.sum(lp[2:3] * lp[3:4], axis=-1, keepdims=True)) + linit)
    def scores(i):
        lo = i * tq
        q = q_ref[lo:lo + tq, :]
        zero = jnp.zeros_like(q)
        return [_scores_t(jnp.where((lane < DIFF_HD) != bool(j), q, zero), k_ref, lo, bias_t)
                for j in range(2)]

    def outputs(i, sc):
        lo = i * tq
        maps = []
        for j in range(2):
            acc = _weighted_values_t(sc[j], vt_sc, lo)
            maps.append(acc[0:dv, :] / acc[dv:dv + 1, :])
        o_t = maps[0] - lam * maps[1]
        o_t = o_t * lax.rsqrt(jnp.mean(o_t * o_t, axis=0, keepdims=True) + EPS)
        o_ref[lo:lo + tq, :] = (o_t * (1.0 - linit)).T.astype(BF16)

    _skewed(s // tq, scores, outputs)


def _diff_attention(q, k, v, lam_params, lambda_init):
    b, s, _ = q.shape

    def spec():
        return pl.BlockSpec((None, s, LANES), lambda i, j: (i, 0, j))

    return pl.pallas_call(
        _diff_attn_kernel,
        grid=(b, DIFF_HEADS),
        in_specs=[spec(), spec(), spec(),
                  pl.BlockSpec((4, DIFF_HD), lambda i, j: (0, 0)),
                  pl.BlockSpec((1, 1), lambda i, j: (0, 0))],
        out_specs=spec(),
        out_shape=jax.ShapeDtypeStruct((b, s, BRANCH_WIDTH), BF16),
        scratch_shapes=[pltpu.VMEM((2 * DIFF_HD + ONES_ROWS, s), BF16)],
        compiler_params=_params("parallel", "parallel"),
        name="diff_attention",
    )(q, k, v, lam_params, jnp.full((1, 1), lambda_init, F32))


def _merge_kernel(x_ref, g_ref, yr_ref, ym_ref, yd_ref, wg_ref, wb_ref, wo_ref, o_ref):
    x = x_ref[...]
    h = _rms(x, g_ref[0:1, :]).astype(BF16)
    mixed = None
    for n, y_ref in enumerate((yr_ref, ym_ref, yd_ref)):
        gate = jax.nn.sigmoid(_dot(h, wg_ref[:, n * D_MODEL:(n + 1) * D_MODEL]))
        term = gate * _dot(y_ref[...], wb_ref[n])
        mixed = term if mixed is None else mixed + term
    y = _dot(mixed.astype(BF16), wo_ref[...])
    o_ref[...] = x + _rms(y, g_ref[1:2, :])


def _merge(x, gains, y_ret, y_mla, y_diff, wg, wb, wo, layer):
    t = x.shape[0]
    tm = TOKEN_TILE

    def rows(w):
        return pl.BlockSpec((tm, w), lambda i: (i, 0))

    return pl.pallas_call(
        _merge_kernel,
        grid=(t // tm,),
        in_specs=[rows(D_MODEL),
                  _resident((2, D_MODEL), layer),
                  rows(BRANCH_WIDTH), rows(BRANCH_WIDTH), rows(BRANCH_WIDTH),
                  _resident((D_MODEL, N_BRANCH * D_MODEL), layer),
                  _resident((N_BRANCH, BRANCH_WIDTH, D_MODEL), layer),
                  _resident((D_MODEL, D_MODEL), layer)],
        out_specs=rows(D_MODEL),
        out_shape=jax.ShapeDtypeStruct((t, D_MODEL), F32),
        compiler_params=_params("parallel"),
        name="merge",
    )(x, gains, y_ret, y_mla, y_diff, wg, wb, wo)


def _memkv_kernel(mem_ref, g_ref, w_ref, o_ref):
    m = _rms(mem_ref[...], g_ref[2:3, :]).astype(BF16)
    o_ref[...] = _dot(m, w_ref[...]).astype(BF16)


def _memkv(mem, gains, wkv):
    b, n, _ = mem.shape
    depth = wkv.shape[0]
    return pl.pallas_call(
        _memkv_kernel,
        grid=(depth, b),
        in_specs=[pl.BlockSpec((None, n, D_MODEL), lambda l, i: (i, 0, 0)),
                  pl.BlockSpec((None, 3, D_MODEL), lambda l, i: (l, 0, 0)),
                  pl.BlockSpec((None, D_MODEL, 2 * D_MODEL), lambda l, i: (l, 0, 0))],
        out_specs=pl.BlockSpec((None, None, n, 2 * D_MODEL), lambda l, i: (l, i, 0, 0)),
        out_shape=jax.ShapeDtypeStruct((depth, b, n, 2 * D_MODEL), BF16),
        compiler_params=_params("parallel", "parallel"),
        name="memkv",
    )(mem, gains, wkv)


def _cross_kernel(x_ref, g_ref, kv_ref, wq_ref, wo_ref, o_ref):
    x = x_ref[...]
    h = _rms(x, g_ref[0:1, :]).astype(BF16)
    q = (_dot(h, wq_ref[...]) * (LOG2E * CROSS_HD ** -0.5)).astype(BF16)
    heads = []
    for hd in range(CROSS_HEADS):
        cols = slice(hd * CROSS_HD, (hd + 1) * CROSS_HD)
        vcols = slice(D_MODEL + hd * CROSS_HD, D_MODEL + (hd + 1) * CROSS_HD)
        sc = _dot_nt(q[:, cols], kv_ref[:, cols])
        p = jnp.exp2(sc - jnp.max(sc, axis=-1, keepdims=True))
        total = jnp.sum(p, axis=-1, keepdims=True)
        heads.append((_dot(p.astype(BF16), kv_ref[:, vcols]) / total).astype(BF16))
    y = _dot(jnp.concatenate(heads, axis=-1), wo_ref[...])
    o_ref[...] = x + _rms(y, g_ref[1:2, :])


def _cross(x, gains, kv, wq, wo, layer):
    b, s, _ = x.shape
    tm = TOKEN_TILE
    n = kv.shape[2]
    return pl.pallas_call(
        _cross_kernel,
        grid=(b, s // tm),
        in_specs=[pl.BlockSpec((None, tm, D_MODEL), lambda i, j: (i, j, 0)),
                  _resident((3, D_MODEL), layer),
                  pl.BlockSpec((None, None, n, 2 * D_MODEL), lambda i, j: (layer, i, 0, 0)),
                  _resident((D_MODEL, D_MODEL), layer),
                  _resident((D_MODEL, D_MODEL), layer)],
        out_specs=pl.BlockSpec((None, tm, D_MODEL), lambda i, j: (i, j, 0)),
        out_shape=jax.ShapeDtypeStruct((b, s, D_MODEL), F32),
        compiler_params=_params("parallel", "parallel"),
        name="cross_attention",
    )(x, gains, kv, wq, wo)


def _pack_inproj(w_in, wq_b, wkv_b):
    depth, d, _ = w_in.shape
    k_rope_at = 1920
    wa = jnp.concatenate(
        [w_in[:, :, :k_rope_at].astype(BF16),
         jnp.zeros((depth, d, MLA_NOPE), BF16),
         w_in[:, :, k_rope_at:k_rope_at + MLA_ROPE].astype(BF16),
         jnp.zeros((depth, d, LANES - MLA_NOPE - MLA_ROPE), BF16),
         w_in[:, :, k_rope_at + MLA_ROPE:GATE_OFFSET].astype(BF16)], axis=2)
    wg = w_in[:, :, GATE_OFFSET:].astype(BF16)
    wq = wq_b.astype(BF16).reshape(depth, MLA_Q_RANK, MLA_HEADS, MLA_NOPE + MLA_ROPE)
    wq = jnp.pad(wq, ((0, 0), (0, 0), (0, 0), (0, LANES - MLA_NOPE - MLA_ROPE)))
    wq = wq.reshape(depth, MLA_Q_RANK, MLA_HEADS * LANES)
    wkv = wkv_b.astype(BF16).reshape(depth, MLA_KV_RANK, MLA_HEADS, MLA_NOPE + MLA_DV)
    wk = jnp.pad(wkv[..., :MLA_NOPE], ((0, 0), (0, 0), (0, 0), (0, LANES - MLA_NOPE)))
    wk = wk.reshape(depth, MLA_KV_RANK, MLA_HEADS * LANES)
    wv = wkv[..., MLA_NOPE:].reshape(depth, MLA_KV_RANK, MLA_HEADS * MLA_DV)
    return wa, wg, wq, wk, wv


def kernel(x, mem, positions, ffn1_norms, ffn1_w13, ffn1_w2, mix_norms, w_in, mla_q_norm,
           mla_kv_norm, mla_wq_b, mla_wkv_b, diff_lambda, w_branch, w_out, cross_norms,
           cross_wq, cross_wkv, cross_wo, ffn2_norms, ffn2_w13, ffn2_w2):
    b, s, d = x.shape
    t = b * s
    depth = w_in.shape[0]
    cos_tab, sin_tab = _rope_tables(positions)

    f1_w13, f1_w2 = ffn1_w13.astype(BF16), ffn1_w2.astype(BF16)
    f2_w13, f2_w2 = ffn2_w13.astype(BF16), ffn2_w2.astype(BF16)
    wa, wg, wqb, wkbk, wkbv = _pack_inproj(w_in, mla_wq_b, mla_wkv_b)
    wb, wo = w_branch.astype(BF16), w_out.astype(BF16)
    c_wq, c_wo = cross_wq.astype(BF16), cross_wo.astype(BF16)
    q_norm = mla_q_norm.reshape(depth, 1, MLA_Q_RANK)
    kv_norm = mla_kv_norm.reshape(depth, 1, MLA_KV_RANK)

    mem_kv = _memkv(mem, cross_norms, cross_wkv.astype(BF16))

    def seq(a):
        return a.reshape(b, s, a.shape[-1])

    xf = x.reshape(t, d)
    for l in range(depth):
        lambda_init = 0.8 - 0.6 * math.exp(-0.3 * l)
        xf = _ffn(xf, ffn1_norms, f1_w13, f1_w2, l)

        (rq, rk, rv, rg, mq, mk, mv, dq, dk, dv) = _inproj(
            xf, mix_norms, wa, q_norm, kv_norm, wqb, wkbk, wkbv, cos_tab, sin_tab, l)
        y_ret = _retention(seq(rq), seq(rk), seq(rv), seq(rg))
        y_mla = _mla_attention(seq(mq), seq(mk), seq(mv))
        y_diff = _diff_attention(seq(dq), seq(dk), seq(dv), diff_lambda[l], lambda_init)
        xf = _merge(xf, mix_norms, y_ret.reshape(t, -1), y_mla.reshape(t, -1),
                    y_diff.reshape(t, -1), wg, wb, wo, l)

        xf = _cross(xf.reshape(b, s, d), cross_norms, mem_kv, c_wq, c_wo, l).reshape(t, d)

        xf = _ffn(xf, ffn2_norms, f2_w13, f2_w2, l)
    return xf.reshape(b, s, d)
```

```python
import functools
import math

import numpy as np
import jax
import jax.numpy as jnp
from jax import lax
from jax.experimental import pallas as pl
from jax.experimental.pallas import tpu as pltpu

F32 = jnp.float32
BF16 = jnp.bfloat16

D_MODEL = 1024
DEPTH = 4
CHUNK = 64
EPS = 1e-6
NEG_INF = -1e30
ROPE_THETA = 500000.0
RET_THETA = 10000.0

RET_HEADS = 4
RET_DK = 64
RET_DV = 128
MLA_HEADS = 8
MLA_Q_RANK = 256
MLA_KV_RANK = 128
MLA_NOPE = 64
MLA_ROPE = 32
MLA_DV = 64
DIFF_HEADS = 4
DIFF_HD = 64
DIFF_ROT = 16
N_BRANCH = 3
BRANCH_WIDTH = 512
CROSS_HEADS = 4
CROSS_HD = 256
D_FF = 2816
GATE_OFFSET = 3488

LANES = 128
VMEM_LIMIT = 56 * 1024 * 1024

TOKEN_TILE = 512
DENSE_TILE = 1024
SUB_TILE = 256
ATTN_TILE = 256
RET_TILE = 256

LOG2E = math.log2(math.e)
RET_LOG_GAMMA =tuple(math.log(1.0 - 2.0 ** (-5.0 - h)) for h in range(RET_HEADS))


def _params(*sem):
    return pltpu.CompilerParams(dimension_semantics=sem, vmem_limit_bytes=VMEM_LIMIT)


def _rms(x, gain=None):
    y = x * lax.rsqrt(jnp.mean(x * x, axis=-1, keepdims=True) + EPS)
    return y if gain is None else y * gain


def _dot(a, b):
    return jnp.dot(a, b, preferred_element_type=F32)


def _dot_nt(a, b):
    return lax.dot_general(a, b, (((1,), (1,)), ((), ())), preferred_element_type=F32)


def _rope(x, cos, sin_signed, half):
    lane = lax.broadcasted_iota(jnp.int32, x.shape, 1)
    first = (lane & (2 * half - 1)) < half
    partner = jnp.where(first, pltpu.roll(x, LANES - half, 1), pltpu.roll(x, half, 1))
    return x * cos + partner * sin_signed


def _rope_patterns():
    inv = np.zeros((3, LANES), np.float32)
    sign = np.zeros((3, LANES), np.float32)
    for i in range(LANES):
        j = i % RET_DK
        inv[0, i] = 1.0 / (RET_THETA ** (np.float32(2 * (j % 32)) / RET_DK))
        sign[0, i] = -1.0 if j < 32 else 1.0
        if 64 <= i < 96:
            j = i - 64
            inv[1, i] = 1.0 / (ROPE_THETA ** (np.float32(2 * (j % 16)) / MLA_ROPE))
            sign[1, i] = -1.0 if j < 16 else 1.0
        j = i % DIFF_HD
        if j < DIFF_ROT:
            inv[2, i] = 1.0 / (ROPE_THETA ** (np.float32(2 * (j % 8)) / DIFF_ROT))
            sign[2, i] = -1.0 if j < 8 else 1.0
    return jnp.asarray(inv), jnp.asarray(sign)


def _rope_table_kernel(pos_ref, inv_ref, sign_ref, cos_ref, sin_ref):
    pos = pos_ref[...].astype(F32)
    for t in range(3):
        ang = pos * inv_ref[t:t + 1, :]
        cos_ref[t] = jnp.cos(ang)
        sin_ref[t] = jnp.sin(ang) * sign_ref[t:t + 1, :]


def _rope_tables(positions):
    t = positions.size
    inv, sign = _rope_patterns()
    tm = TOKEN_TILE
    tab = jax.ShapeDtypeStruct((3, t, LANES), F32)
    return pl.pallas_call(
        _rope_table_kernel,
        grid=(t // tm,),
        in_specs=[pl.BlockSpec((tm, 1), lambda i: (i, 0)),
                  pl.BlockSpec((3, LANES), lambda i: (0, 0)),
                  pl.BlockSpec((3, LANES), lambda i: (0, 0))],
        out_specs=[pl.BlockSpec((3, tm, LANES), lambda i: (0, i, 0)),
                   pl.BlockSpec((3, tm, LANES), lambda i: (0, i, 0))],
        out_shape=[tab, tab],
        compiler_params=_params("parallel"),
        name="rope_tables",
    )(positions.reshape(t, 1), inv, sign)


def _wavefront(n_items, stages):
    state = [None] * n_items
    for wave in range(n_items + len(stages) - 1):
        for s in reversed(range(len(stages))):
            n = wave - s
            if 0 <= n < n_items:
                state[n] = stages[s](n, state[n])


def _row_block(n):
    return pl.ds(n * SUB_TILE, SUB_TILE)


def _ffn_kernel(x_ref, g_ref, w13_ref, w2_ref, o_ref):
    def pre_norm(n, _):
        return _rms(x_ref[_row_block(n), :], g_ref[0:1, :]).astype(BF16)

    def hidden(n, h):
        gate = _dot(h, w13_ref[:, :D_FF])
        up = _dot(h, w13_ref[:, D_FF:])
        return (gate * jax.nn.sigmoid(gate) * up).astype(BF16)

    def project(n, act):
        y = _dot(act, w2_ref[...])
        o_ref[_row_block(n), :] = x_ref[_row_block(n), :] + 0.5 * _rms(y, g_ref[1:2, :])

    _wavefront(x_ref.shape[0] // SUB_TILE, (pre_norm, hidden, project))


def _resident(shape, layer):
    zeros = (0,) * len(shape)
    return pl.BlockSpec((None,) + tuple(shape), lambda *_: (layer,) + zeros,
                        pipeline_mode=pl.Buffered(1))


def _ffn(x, gains, w13, w2, layer):
    t = x.shape[0]
    tm = DENSE_TILE
    return pl.pallas_call(
        _ffn_kernel,
        grid=(t // tm,),
        in_specs=[pl.BlockSpec((tm, D_MODEL), lambda i: (i, 0)),
                  _resident((2, D_MODEL), layer),
                  _resident((D_MODEL, 2 * D_FF), layer),
                  _resident((D_FF, D_MODEL), layer)],
        out_specs=pl.BlockSpec((tm, D_MODEL), lambda i: (i, 0)),
        out_shape=jax.ShapeDtypeStruct((t, D_MODEL), F32),
        compiler_params=_params("parallel"),
        name="ffn",
    )(x, gains, w13, w2)


_C_RQ, _C_RK, _C_RV, _C_RG = 0, 256, 512, 1024
_C_MQ, _C_MKV = 1536, 1792
_C_DQ, _C_DK, _C_DV = 2048, 2560, 3072
PROJ_WIDTH = 3584


def _inproj_kernel(x_ref, g_ref, wa_ref, qn_ref, kvn_ref, wqb_ref, wkbk_ref, wkbv_ref,
                   cos_ref, sin_ref,
                   rq_ref, rk_ref, rv_ref, rg_ref, mq_ref, mk_ref, mv_ref,
                   dq_ref, dk_ref, dv_ref):
    h = _rms(x_ref[...], g_ref[0:1, :]).astype(BF16)

    def proj(start, width):
        return _dot(h, wa_ref[:, start:start + width])

    def rope_store(dst, val, table, half, scale):
        cos, sin = cos_ref[table], sin_ref[table]
        for c in range(val.shape[1] // LANES):
            sl = slice(c * LANES, (c + 1) * LANES)
            r = _rope(val[:, sl], cos, sin, half)
            if scale != 1.0:
                r = r * scale
            dst[:, sl] = r.astype(BF16)

    def cast_store(dst):
        def store(val):
            dst[...] = val.astype(BF16)
        return store

    def silu_store(val):
        rg_ref[...] = (val * jax.nn.sigmoid(val)).astype(BF16)

    low_rank = {}

    def keep_cq(val):
        low_rank["cq"] = _rms(val, qn_ref[...]).astype(BF16)

    def keep_kv(val):
        low_rank["ckv"] = _rms(val[:, :MLA_KV_RANK], kvn_ref[...]).astype(BF16)
        low_rank["k_rope"] = _rope(val[:, MLA_KV_RANK:], cos_ref[1], sin_ref[1], MLA_ROPE // 2)

    def key_store(k_nope):
        for hd in range(MLA_HEADS):
            sl = slice(hd * LANES, (hd + 1) * LANES)
            mk_ref[:, sl] = (k_nope[:, sl] + low_rank["k_rope"]).astype(BF16)

    def roped(dst, table, half, scale):
        return lambda val: rope_store(dst, val, table, half, scale)

    items = (
        (lambda: proj(_C_RQ, 256), roped(rq_ref, 0, RET_DK // 2, RET_DK ** -0.5)),
        (lambda: proj(_C_MQ, MLA_Q_RANK), keep_cq),
        (lambda: proj(_C_RK, 256), roped(rk_ref, 0, RET_DK // 2, 1.0)),
        (lambda: proj(_C_MKV, MLA_KV_RANK + LANES), keep_kv),
        (lambda: _dot(low_rank["cq"], wqb_ref[...]),
         roped(mq_ref, 1, MLA_ROPE // 2, LOG2E * (MLA_NOPE + MLA_ROPE) ** -0.5)),
        (lambda: proj(_C_RV, 512), cast_store(rv_ref)),
        (lambda: _dot(low_rank["ckv"], wkbk_ref[...]), key_store),
        (lambda: proj(_C_RG, 512), silu_store),
        (lambda: _dot(low_rank["ckv"], wkbv_ref[...]), cast_store(mv_ref)),
        (lambda: proj(_C_DQ, 512), roped(dq_ref, 2, DIFF_ROT // 2, LOG2E * DIFF_HD ** -0.5)),
        (lambda: proj(_C_DK, 512), roped(dk_ref, 2, DIFF_ROT // 2, 1.0)),
        (lambda: proj(_C_DV, 512), cast_store(dv_ref)),
    )
    _skewed(len(items), lambda n: items[n][0](), lambda n, val: items[n][1](val))


def _inproj(x, gains, wa, q_norm, kv_norm, wqb, wkbk, wkbv, cos_tab, sin_tab, layer):
    t = x.shape[0]
    tm = TOKEN_TILE
    widths = (256, 256, 512, 512, 1024, 1024, 512, 512, 512, 512)

    return pl.pallas_call(
        _inproj_kernel,
        grid=(t // tm,),
        in_specs=[pl.BlockSpec((tm, D_MODEL), lambda i: (i, 0)),
                  _resident((2, D_MODEL), layer),
                  _resident((D_MODEL, PROJ_WIDTH), layer),
                  _resident((1, MLA_Q_RANK), layer),
                  _resident((1, MLA_KV_RANK), layer),
                  _resident((MLA_Q_RANK, MLA_HEADS * LANES), layer),
                  _resident((MLA_KV_RANK, MLA_HEADS * LANES), layer),
                  _resident((MLA_KV_RANK, MLA_HEADS * MLA_DV), layer),
                  pl.BlockSpec((3, tm, LANES), lambda i: (0, i, 0)),
                  pl.BlockSpec((3, tm, LANES), lambda i: (0, i, 0))],
        out_specs=[pl.BlockSpec((tm, w), lambda i: (i, 0)) for w in widths],
        out_shape=[jax.ShapeDtypeStruct((t, w), BF16) for w in widths],
        compiler_params=_params("parallel"),
        name="inproj",
    )(x, gains, wa, q_norm, kv_norm, wqb, wkbk, wkbv, cos_tab, sin_tab)


def _retention_kernel(q_ref, k_ref, v_ref, g_ref, o_ref, state_sc):
    r = RET_TILE

    @pl.when(pl.program_id(1) == 0)
    def _():
        state_sc[...] = jnp.zeros_like(state_sc)

    row = lax.broadcasted_iota(jnp.int32, (r, r), 0)
    col = lax.broadcasted_iota(jnp.int32, (r, r), 1)
    allowed = (col >> 6) <= (row >> 6)
    dist = jnp.abs(row - col).astype(F32)
    lane = lax.broadcasted_iota(jnp.int32, (r, LANES), 1)
    first_head = lane < RET_DK
    n_local = lax.broadcasted_iota(jnp.int32, (r, LANES), 0).astype(F32)
    s_row = lax.broadcasted_iota(jnp.int32, (LANES, 2 * RET_DV), 0)
    s_col = lax.broadcasted_iota(jnp.int32, (LANES, 2 * RET_DV), 1)
    same_head = (s_row < RET_DK) == (s_col < RET_DV)

    for p in range(RET_HEADS // 2):
        lg0, lg1 = RET_LOG_GAMMA[2 * p], RET_LOG_GAMMA[2 * p + 1]
        q = q_ref[:, p * LANES:(p + 1) * LANES]
        k = k_ref[:, p * LANES:(p + 1) * LANES]
        v = v_ref[:, 2 * p * RET_DV:2 * (p + 1) * RET_DV]
        lg_lane = jnp.where(first_head, lg0, lg1)
        state = state_sc[p]

        q_dec = (q.astype(F32) * jnp.exp(lg_lane * (n_local + 1.0))).astype(BF16)
        o_cross = _dot(q_dec, state.astype(BF16))

        for hh in range(2):
            lg = lg1 if hh else lg0
            qm = jnp.where(first_head != bool(hh), q, jnp.zeros_like(q))
            decay = jnp.where(allowed, jnp.exp(lg * dist), 0.0)
            scores = (_dot_nt(qm, k) * decay).astype(BF16)
            sl = slice(hh * RET_DV, (hh + 1) * RET_DV)
            o = _dot(scores, v[:, sl]) + o_cross[:, sl]
            gsl = slice((2 * p + hh) * RET_DV, (2 * p + hh + 1) * RET_DV)
            o_ref[:, gsl] = (_rms(o) * g_ref[:, gsl].astype(F32)).astype(BF16)

        k_dec = k.astype(F32) * jnp.exp(lg_lane * (r - 1.0 - n_local))
        kv = _dot(k_dec.T.astype(BF16), v)
        block_decay = jnp.where(s_col < RET_DV, math.exp(lg0 * r), math.exp(lg1 * r))
        state_sc[p] = state * block_decay + jnp.where(same_head, kv, 0.0)


def _retention(q, k, v, g):
    b, s, _ = q.shape
    r = RET_TILE

    def spec(w):
        return pl.BlockSpec((None, r, w), lambda i, j: (i, j, 0))

    return pl.pallas_call(
        _retention_kernel,
        grid=(b, s // r),
        in_specs=[spec(256), spec(256), spec(512), spec(512)],
        out_specs=spec(512),
        out_shape=jax.ShapeDtypeStruct((b, s, BRANCH_WIDTH), BF16),
        scratch_shapes=[pltpu.VMEM((RET_HEADS // 2, LANES, 2 * RET_DV), F32)],
        compiler_params=_params("parallel", "arbitrary"),
        name="retention",
    )(q, k, v, g)


ONES_ROWS = 16


def _scores_t(q, k_ref, lo, bias_t):
    tq = q.shape[0]
    s_diag = _dot_nt(k_ref[lo:lo + tq, :], q) + bias_t
    m = jnp.max(s_diag, axis=0, keepdims=True)
    s_past = None
    if lo > 0:
        s_past = _dot_nt(k_ref[0:lo, :], q)
        m = jnp.maximum(m, jnp.max(s_past, axis=0, keepdims=True))
    return s_diag, s_past, m


def _softmax_weights_t(scores):
    s_diag, s_past, m = scores
    p_diag = jnp.exp2(s_diag - m).astype(BF16)
    p_past = None if s_past is None else jnp.exp2(s_past - m).astype(BF16)
    return p_diag, p_past


def _weighted_values_t(weights, vt_ref, lo):
    p_diag, p_past = weights
    tq = p_diag.shape[1]
    out = _dot(vt_ref[:, lo:lo + tq], p_diag)
    if p_past is not None:
        out = out + _dot(vt_ref[:, 0:lo], p_past)
    return out


def _skewed(n_items, first_stage, second_stage):
    pending = first_stage(0)
    for n in range(n_items):
        ahead = first_stage(n + 1) if n + 1 < n_items else None
        second_stage(n, pending)
        pending = ahead


def _chunk_bias_t(n):
    key = lax.broadcasted_iota(jnp.int32, (n, n), 0)
    qry = lax.broadcasted_iota(jnp.int32, (n, n), 1)
    return jnp.where((key >> 6) <= (qry >> 6), 0.0, NEG_INF).astype(F32)


def _mla_attn_kernel(q_ref, k_ref, v_ref, o_ref, vt_sc):
    tq = ATTN_TILE
    s = q_ref.shape[0]
    vt = v_ref[...].astype(F32).T
    for hh in range(2):
        vt_sc[hh, 0:MLA_DV, :] = vt[hh * MLA_DV:(hh + 1) * MLA_DV, :].astype(BF16)
        vt_sc[hh, MLA_DV:, :] = jnp.ones((ONES_ROWS, s), BF16)
    bias_t = _chunk_bias_t(tq)

    def scores(i, _):
        lo = i * tq
        return [_scores_t(q_ref[lo:lo + tq, hh * LANES:(hh + 1) * LANES],
                          k_ref.at[:, hh * LANES:(hh + 1) * LANES], lo, bias_t) for hh in range(2)]

    def weights(i, sc):
        return [_softmax_weights_t(sc[hh]) for hh in range(2)]

    def outputs(i, w):
        lo = i * tq
        halves = []
        for hh in range(2):
            acc = _weighted_values_t(w[hh], vt_sc.at[hh], lo)
            halves.append(acc[0:MLA_DV, :] / acc[MLA_DV:MLA_DV + 1, :])
        o_ref[lo:lo + tq, :] = jnp.concatenate(halves, axis=0).T.astype(BF16)

    _wavefront(s // tq, (scores, weights, outputs))


def _mla_attention(q, k, v):
    b, s, _ = q.shape
    pairs = MLA_HEADS // 2
    return pl.pallas_call(
        _mla_attn_kernel,
        grid=(b, pairs),
        in_specs=[pl.BlockSpec((None, s, 2 * LANES), lambda i, j: (i, 0, j)),
                  pl.BlockSpec((None, s, 2 * LANES), lambda i, j: (i, 0, j)),
                  pl.BlockSpec((None, s, LANES), lambda i, j: (i, 0, j))],
        out_specs=pl.BlockSpec((None, s, LANES), lambda i, j: (i, 0, j)),
        out_shape=jax.ShapeDtypeStruct((b, s, BRANCH_WIDTH), BF16),
        scratch_shapes=[pltpu.VMEM((2, MLA_DV + ONES_ROWS, s), BF16)],
        compiler_params=_params("parallel", "parallel"),
        name="mla_attention",
    )(q, k, v)


def _diff_attn_kernel(q_ref, k_ref, v_ref, lam_ref, linit_ref, o_ref, vt_sc):
    tq = ATTN_TILE
    s = q_ref.shape[0]
    dv = 2 * DIFF_HD
    vt_sc[0:dv, :] = v_ref[...].astype(F32).T.astype(BF16)
    vt_sc[dv:, :] = jnp.ones((ONES_ROWS, s), BF16)
    bias_t = _chunk_bias_t(tq)
    lane = lax.broadcasted_iota(jnp.int32, (tq, LANES), 1)
    lp = lam_ref[...]
    linit = linit_ref[...]
    lam = (jnp.exp(jnp.sum(lp[0:1] * lp[1:2], axis=-1, keepdims=True))
           - jnp.exp(jnp.sum(lp[2:3] * lp[3:4], axis=-1, keepdims=True)) + linit)
    def scores(i, _):
        lo = i * tq
        q = q_ref[lo:lo + tq, :]
        zero = jnp.zeros_like(q)
        return [_scores_t(jnp.where((lane < DIFF_HD) != bool(j), q, zero), k_ref, lo, bias_t)
                for j in range(2)]

    def weights(i, sc):
        return [_softmax_weights_t(sc[j]) for j in range(2)]

    def outputs(i, w):
        lo = i * tq
        maps = []
        for j in range(2):
            acc = _weighted_values_t(w[j], vt_sc, lo)
            maps.append(acc[0:dv, :] / acc[dv:dv + 1, :])
        o_t = maps[0] - lam * maps[1]
        o_t = o_t * lax.rsqrt(jnp.mean(o_t * o_t, axis=0, keepdims=True) + EPS)
        o_ref[lo:lo + tq, :] = (o_t * (1.0 - linit)).T.astype(BF16)

    _wavefront(s // tq, (scores, weights, outputs))


def _diff_attention(q, k, v, lam_params, lambda_init):
    b, s, _ = q.shape

    def spec():
        return pl.BlockSpec((None, s, LANES), lambda i, j: (i, 0, j))

    return pl.pallas_call(
        _diff_attn_kernel,
        grid=(b, DIFF_HEADS),
        in_specs=[spec(), spec(), spec(),
                  pl.BlockSpec((4, DIFF_HD), lambda i, j: (0, 0)),
                  pl.BlockSpec((1, 1), lambda i, j: (0, 0))],
        out_specs=spec(),
        out_shape=jax.ShapeDtypeStruct((b, s, BRANCH_WIDTH), BF16),
        scratch_shapes=[pltpu.VMEM((2 * DIFF_HD + ONES_ROWS, s), BF16)],
        compiler_params=_params("parallel", "parallel"),
        name="diff_attention",
    )(q, k, v, lam_params, jnp.full((1, 1), lambda_init, F32))


def _merge_kernel(x_ref, g_ref, yr_ref, ym_ref, yd_ref, wg_ref, wb_ref, wo_ref, o_ref):
    def pre_norm(n, _):
        return _rms(x_ref[_row_block(n), :], g_ref[0:1, :]).astype(BF16)

    def gated_sum(n, h):
        mixed = None
        for br, y_ref in enumerate((yr_ref, ym_ref, yd_ref)):
            gate = jax.nn.sigmoid(_dot(h, wg_ref[:, br * D_MODEL:(br + 1) * D_MODEL]))
            term = gate * _dot(y_ref[_row_block(n), :], wb_ref[br])
            mixed = term if mixed is None else mixed + term
        return mixed.astype(BF16)

    def project(n, mixed):
        y = _dot(mixed, wo_ref[...])
        o_ref[_row_block(n), :] = x_ref[_row_block(n), :] + _rms(y, g_ref[1:2, :])

    _wavefront(x_ref.shape[0] // SUB_TILE, (pre_norm, gated_sum, project))


def _merge(x, gains, y_ret, y_mla, y_diff, wg, wb, wo, layer):
    t = x.shape[0]
    tm = DENSE_TILE

    def rows(w):
        return pl.BlockSpec((tm, w), lambda i: (i, 0))

    return pl.pallas_call(
        _merge_kernel,
        grid=(t // tm,),
        in_specs=[rows(D_MODEL),
                  _resident((2, D_MODEL), layer),
                  rows(BRANCH_WIDTH), rows(BRANCH_WIDTH), rows(BRANCH_WIDTH),
                  _resident((D_MODEL, N_BRANCH * D_MODEL), layer),
                  _resident((N_BRANCH, BRANCH_WIDTH, D_MODEL), layer),
                  _resident((D_MODEL, D_MODEL), layer)],
        out_specs=rows(D_MODEL),
        out_shape=jax.ShapeDtypeStruct((t, D_MODEL), F32),
        compiler_params=_params("parallel"),
        name="merge",
    )(x, gains, y_ret, y_mla, y_diff, wg, wb, wo)


def _memkv_kernel(mem_ref, g_ref, w_ref, o_ref):
    m = _rms(mem_ref[...], g_ref[2:3, :]).astype(BF16)
    o_ref[...] = _dot(m, w_ref[...]).astype(BF16)


def _memkv(mem, gains, wkv):
    b, n, _ = mem.shape
    depth = wkv.shape[0]
    return pl.pallas_call(
        _memkv_kernel,
        grid=(depth, b),
        in_specs=[pl.BlockSpec((None, n, D_MODEL), lambda l, i: (i, 0, 0)),
                  pl.BlockSpec((None, 3, D_MODEL), lambda l, i: (l, 0, 0)),
                  pl.BlockSpec((None, D_MODEL, 2 * D_MODEL), lambda l, i: (l, 0, 0))],
        out_specs=pl.BlockSpec((None, None, n, 2 * D_MODEL), lambda l, i: (l, i, 0, 0)),
        out_shape=jax.ShapeDtypeStruct((depth, b, n, 2 * D_MODEL), BF16),
        compiler_params=_params("parallel", "parallel"),
        name="memkv",
    )(mem, gains, wkv)


def _cross_kernel(x_ref, g_ref, kv_ref, wq_ref, wo_ref, o_ref):
    def query(n, _):
        h = _rms(x_ref[_row_block(n), :], g_ref[0:1, :]).astype(BF16)
        return (_dot(h, wq_ref[...]) * (LOG2E * CROSS_HD ** -0.5)).astype(BF16)

    def attend(n, q):
        heads = []
        for hd in range(CROSS_HEADS):
            cols = slice(hd * CROSS_HD, (hd + 1) * CROSS_HD)
            vcols = slice(D_MODEL + hd * CROSS_HD, D_MODEL + (hd + 1) * CROSS_HD)
            sc = _dot_nt(q[:, cols], kv_ref[:, cols])
            p = jnp.exp2(sc - jnp.max(sc, axis=-1, keepdims=True))
            total = jnp.sum(p, axis=-1, keepdims=True)
            heads.append((_dot(p.astype(BF16), kv_ref[:, vcols]) / total).astype(BF16))
        return jnp.concatenate(heads, axis=-1)

    def project(n, o):
        y = _dot(o, wo_ref[...])
        o_ref[_row_block(n), :] = x_ref[_row_block(n), :] + _rms(y, g_ref[1:2, :])

    _wavefront(x_ref.shape[0] // SUB_TILE, (query, attend, project))


def _cross(x, gains, kv, wq, wo, layer):
    b, s, _ = x.shape
    tm = DENSE_TILE
    n = kv.shape[2]
    return pl.pallas_call(
        _cross_kernel,
        grid=(b, s // tm),
        in_specs=[pl.BlockSpec((None, tm, D_MODEL), lambda i, j: (i, j, 0)),
                  _resident((3, D_MODEL), layer),
                  pl.BlockSpec((None, None, n, 2 * D_MODEL), lambda i, j: (layer, i, 0, 0)),
                  _resident((D_MODEL, D_MODEL), layer),
                  _resident((D_MODEL, D_MODEL), layer)],
        out_specs=pl.BlockSpec((None, tm, D_MODEL), lambda i, j: (i, j, 0)),
        out_shape=jax.ShapeDtypeStruct((b, s, D_MODEL), F32),
        compiler_params=_params("parallel", "parallel"),
        name="cross_attention",
    )(x, gains, kv, wq, wo)


def _pack_inproj(w_in, wq_b, wkv_b):
    depth, d, _ = w_in.shape
    k_rope_at = 1920
    wa = jnp.concatenate(
        [w_in[:, :, :k_rope_at].astype(BF16),
         jnp.zeros((depth, d, MLA_NOPE), BF16),
         w_in[:, :, k_rope_at:k_rope_at + MLA_ROPE].astype(BF16),
         jnp.zeros((depth, d, LANES - MLA_NOPE - MLA_ROPE), BF16),
         w_in[:, :, k_rope_at + MLA_ROPE:GATE_OFFSET].astype(BF16)], axis=2)
    wg = w_in[:, :, GATE_OFFSET:].astype(BF16)
    wq = wq_b.astype(BF16).reshape(depth, MLA_Q_RANK, MLA_HEADS, MLA_NOPE + MLA_ROPE)
    wq = jnp.pad(wq, ((0, 0), (0, 0), (0, 0), (0, LANES - MLA_NOPE - MLA_ROPE)))
    wq = wq.reshape(depth, MLA_Q_RANK, MLA_HEADS * LANES)
    wkv = wkv_b.astype(BF16).reshape(depth, MLA_KV_RANK, MLA_HEADS, MLA_NOPE + MLA_DV)
    wk = jnp.pad(wkv[..., :MLA_NOPE], ((0, 0), (0, 0), (0, 0), (0, LANES - MLA_NOPE)))
    wk = wk.reshape(depth, MLA_KV_RANK, MLA_HEADS * LANES)
    wv = wkv[..., MLA_NOPE:].reshape(depth, MLA_KV_RANK, MLA_HEADS * MLA_DV)
    return wa, wg, wq, wk, wv


def kernel(x, mem, positions, ffn1_norms, ffn1_w13, ffn1_w2, mix_norms, w_in, mla_q_norm,
           mla_kv_norm, mla_wq_b, mla_wkv_b, diff_lambda, w_branch, w_out, cross_norms,
           cross_wq, cross_wkv, cross_wo, ffn2_norms, ffn2_w13, ffn2_w2):
    b, s, d = x.shape
    t = b * s
    depth = w_in.shape[0]
    cos_tab, sin_tab = _rope_tables(positions)

    f1_w13, f1_w2 = ffn1_w13.astype(BF16), ffn1_w2.astype(BF16)
    f2_w13, f2_w2 = ffn2_w13.astype(BF16), ffn2_w2.astype(BF16)
    wa, wg, wqb, wkbk, wkbv = _pack_inproj(w_in, mla_wq_b, mla_wkv_b)
    wb, wo = w_branch.astype(BF16), w_out.astype(BF16)
    c_wq, c_wo = cross_wq.astype(BF16), cross_wo.astype(BF16)
    q_norm = mla_q_norm.reshape(depth, 1, MLA_Q_RANK)
    kv_norm = mla_kv_norm.reshape(depth, 1, MLA_KV_RANK)

    mem_kv = _memkv(mem, cross_norms, cross_wkv.astype(BF16))

    def seq(a):
        return a.reshape(b, s, a.shape[-1])

    xf = x.reshape(t, d)
    for l in range(depth):
        lambda_init = 0.8 - 0.6 * math.exp(-0.3 * l)
        xf = _ffn(xf, ffn1_norms, f1_w13, f1_w2, l)

        (rq, rk, rv, rg, mq, mk, mv, dq, dk, dv) = _inproj(
            xf, mix_norms, wa, q_norm, kv_norm, wqb, wkbk, wkbv, cos_tab, sin_tab, l)
        y_ret = _retention(seq(rq), seq(rk), seq(rv), seq(rg))
        y_mla = _mla_attention(seq(mq), seq(mk), seq(mv))
        y_diff = _diff_attention(seq(dq), seq(dk), seq(dv), diff_lambda[l], lambda_init)
        xf = _merge(xf, mix_norms, y_ret.reshape(t, -1), y_mla.reshape(t, -1),
                    y_diff.reshape(t, -1), wg, wb, wo, l)

        xf = _cross(xf.reshape(b, s, d), cross_norms, mem_kv, c_wq, c_wo, l).reshape(t, d)

        xf = _ffn(xf, ffn2_norms, f2_w13, f2_w2, l)
    return xf.reshape(b, s, d)
```

```python
import functools
import math

import numpy as np
import jax
import jax.numpy as jnp
from jax import lax
from jax.experimental import pallas as pl
from jax.experimental.pallas import tpu as pltpu

F32 = jnp.float32
BF16 = jnp.bfloat16

D_MODEL = 1024
DEPTH = 4
CHUNK = 64
EPS = 1e-6
NEG_INF = -1e30
ROPE_THETA = 500000.0
RET_THETA = 10000.0

RET_HEADS = 4
RET_DK = 64
RET_DV = 128
MLA_HEADS = 8
MLA_Q_RANK = 256
MLA_KV_RANK = 128
MLA_NOPE = 64
MLA_ROPE = 32
MLA_DV = 64
DIFF_HEADS = 4
DIFF_HD = 64
DIFF_ROT = 16
N_BRANCH = 3
BRANCH_WIDTH = 512
CROSS_HEADS = 4
CROSS_HD = 256
D_FF = 2816
GATE_OFFSET = 3488

LANES = 128
VMEM_LIMIT = 56 * 1024 * 1024

TOKEN_TILE = 512
DENSE_TILE = 1024
SUB_TILE = 256
ATTN_TILE = 256
RET_TILE = 256

LOG2E = math.log2(math.e)
RET_LOG_GAMMA =tuple(math.log(1.0 - 2.0 ** (-5.0 - h)) for h in range(RET_HEADS))


def _params(*sem):
    return pltpu.CompilerParams(dimension_semantics=sem, vmem_limit_bytes=VMEM_LIMIT)


def _rms(x, gain=None):
    y = x * lax.rsqrt(jnp.mean(x * x, axis=-1, keepdims=True) + EPS)
    return y if gain is None else y * gain


def _dot(a, b):
    return jnp.dot(a, b, preferred_element_type=F32)


def _dot_nt(a, b):
    return lax.dot_general(a, b, (((1,), (1,)), ((), ())), preferred_element_type=F32)


def _rope(x, cos, sin_signed, half):
    lane = lax.broadcasted_iota(jnp.int32, x.shape, 1)
    first = (lane & (2 * half - 1)) < half
    partner = jnp.where(first, pltpu.roll(x, LANES - half, 1), pltpu.roll(x, half, 1))
    return x * cos + partner * sin_signed


def _rope_patterns():
    inv = np.zeros((1, LANES), np.float32)
    inv[0, 0:32] = 1.0 / (RET_THETA ** (np.arange(0, RET_DK, 2, dtype=np.float32) / RET_DK))
    inv[0, 32:48] = 1.0 / (ROPE_THETA ** (np.arange(0, MLA_ROPE, 2, dtype=np.float32) / MLA_ROPE))
    inv[0, 48:56] = 1.0 / (ROPE_THETA ** (np.arange(0, DIFF_ROT, 2, dtype=np.float32) / DIFF_ROT))
    shift = np.full((3, LANES), -1, np.int32)
    sign = np.zeros((3, LANES), np.float32)
    for i in range(LANES):
        j = i % RET_DK
        shift[0, i] = (i - j % 32) % LANES
        sign[0, i] = -1.0 if j < 32 else 1.0
        if 64 <= i < 96:
            j = i - 64
            shift[1, i] = (i - (32 + j % 16)) % LANES
            sign[1, i] = -1.0 if j < 16 else 1.0
        j = i % DIFF_HD
        if j < DIFF_ROT:
            shift[2, i] = (i - (48 + j % 8)) % LANES
            sign[2, i] = -1.0 if j < 8 else 1.0
    return inv, shift, sign


_ROPE_INV, _ROPE_SHIFT, _ROPE_SIGN = _rope_patterns()


def _rope_table_kernel(pos_ref, inv_ref, shift_ref, sign_ref, cos_ref, sin_ref):
    ang = pos_ref[...].astype(F32) * inv_ref[...]
    packed = {"cos": jnp.cos(ang), "sin": jnp.sin(ang)}
    rolled = {}

    def moved(name, k):
        if k == 0:
            return packed[name]
        if (name, k) not in rolled:
            rolled[name, k] = pltpu.roll(packed[name], k, 1)
        return rolled[name, k]

    for t in range(3):
        lane_shift = shift_ref[t:t + 1, :]
        cos_t = jnp.ones_like(ang)
        sin_t = jnp.zeros_like(ang)
        for k in sorted(set(int(v) for v in _ROPE_SHIFT[t] if v >= 0)):
            cos_t = jnp.where(lane_shift == k, moved("cos", k), cos_t)
            sin_t = jnp.where(lane_shift == k, moved("sin", k), sin_t)
        cos_ref[t] = cos_t
        sin_ref[t] = sin_t * sign_ref[t:t + 1, :]


def _rope_tables(positions):
    t = positions.size
    tm = TOKEN_TILE
    tab = jax.ShapeDtypeStruct((3, t, LANES), F32)
    return pl.pallas_call(
        _rope_table_kernel,
        grid=(t // tm,),
        in_specs=[pl.BlockSpec((tm, 1), lambda i: (i, 0)),
                  pl.BlockSpec((1, LANES), lambda i: (0, 0)),
                  pl.BlockSpec((3, LANES), lambda i: (0, 0)),
                  pl.BlockSpec((3, LANES), lambda i: (0, 0))],
        out_specs=[pl.BlockSpec((3, tm, LANES), lambda i: (0, i, 0)),
                   pl.BlockSpec((3, tm, LANES), lambda i: (0, i, 0))],
        out_shape=[tab, tab],
        compiler_params=_params("parallel"),
        name="rope_tables",
    )(positions.reshape(t, 1), jnp.asarray(_ROPE_INV), jnp.asarray(_ROPE_SHIFT),
      jnp.asarray(_ROPE_SIGN))


def _wavefront(n_items, stages):
    state = [None] * n_items
    for wave in range(n_items + len(stages) - 1):
        for s in reversed(range(len(stages))):
            n = wave - s
            if 0 <= n < n_items:
                state[n] = stages[s](n, state[n])


def _row_block(n):
    return pl.ds(n * SUB_TILE, SUB_TILE)


def _ffn_kernel(x_ref, g_ref, w13_ref, w2_ref, o_ref):
    def pre_norm(n, _):
        return _rms(x_ref[_row_block(n), :], g_ref[0:1, :]).astype(BF16)

    def hidden(n, h):
        gate = _dot(h, w13_ref[:, :D_FF])
        up = _dot(h, w13_ref[:, D_FF:])
        return (gate * jax.nn.sigmoid(gate) * up).astype(BF16)

    def project(n, act):
        y = _dot(act, w2_ref[...])
        o_ref[_row_block(n), :] = x_ref[_row_block(n), :] + 0.5 * _rms(y, g_ref[1:2, :])

    _wavefront(x_ref.shape[0] // SUB_TILE, (pre_norm, hidden, project))


def _resident(shape, layer):
    zeros = (0,) * len(shape)
    return pl.BlockSpec((None,) + tuple(shape), lambda *_: (layer,) + zeros,
                        pipeline_mode=pl.Buffered(1))


def _ffn(x, gains, w13, w2, layer):
    t = x.shape[0]
    tm = DENSE_TILE
    return pl.pallas_call(
        _ffn_kernel,
        grid=(t // tm,),
        in_specs=[pl.BlockSpec((tm, D_MODEL), lambda i: (i, 0)),
                  _resident((2, D_MODEL), layer),
                  _resident((D_MODEL, 2 * D_FF), layer),
                  _resident((D_FF, D_MODEL), layer)],
        out_specs=pl.BlockSpec((tm, D_MODEL), lambda i: (i, 0)),
        out_shape=jax.ShapeDtypeStruct((t, D_MODEL), F32),
        compiler_params=_params("parallel"),
        name="ffn",
    )(x, gains, w13, w2)


_C_RQ, _C_RK, _C_RV, _C_RG = 0, 256, 512, 1024
_C_MQ, _C_MKV = 1536, 1792
_C_DQ, _C_DK, _C_DV = 2048, 2560, 3072
PROJ_WIDTH = 3584


def _inproj_kernel(x_ref, g_ref, wa_ref, qn_ref, kvn_ref, wqb_ref, wkbk_ref, wkbv_ref,
                   cos_ref, sin_ref,
                   rq_ref, rk_ref, rv_ref, rg_ref, mq_ref, mk_ref, mv_ref,
                   dq_ref, dk_ref, dv_ref):
    h = _rms(x_ref[...], g_ref[0:1, :]).astype(BF16)

    def proj(start, width):
        return _dot(h, wa_ref[:, start:start + width])

    def rope_store(dst, val, table, half, scale):
        cos, sin = cos_ref[table], sin_ref[table]
        for c in range(val.shape[1] // LANES):
            sl = slice(c * LANES, (c + 1) * LANES)
            r = _rope(val[:, sl], cos, sin, half)
            if scale != 1.0:
                r = r * scale
            dst[:, sl] = r.astype(BF16)

    def cast_store(dst):
        def store(val):
            dst[...] = val.astype(BF16)
        return store

    def silu_store(val):
        rg_ref[...] = (val * jax.nn.sigmoid(val)).astype(BF16)

    low_rank = {}

    def keep_cq(val):
        low_rank["cq"] = _rms(val, qn_ref[...]).astype(BF16)

    def keep_kv(val):
        low_rank["ckv"] = _rms(val[:, :MLA_KV_RANK], kvn_ref[...]).astype(BF16)
        low_rank["k_rope"] = _rope(val[:, MLA_KV_RANK:], cos_ref[1], sin_ref[1], MLA_ROPE // 2)

    def key_store(k_nope):
        for hd in range(MLA_HEADS):
            sl = slice(hd * LANES, (hd + 1) * LANES)
            mk_ref[:, sl] = (k_nope[:, sl] + low_rank["k_rope"]).astype(BF16)

    def roped(dst, table, half, scale):
        return lambda val: rope_store(dst, val, table, half, scale)

    items = (
        (lambda: proj(_C_RQ, 256), roped(rq_ref, 0, RET_DK // 2, RET_DK ** -0.5)),
        (lambda: proj(_C_MQ, MLA_Q_RANK), keep_cq),
        (lambda: proj(_C_RK, 256), roped(rk_ref, 0, RET_DK // 2, 1.0)),
        (lambda: proj(_C_MKV, MLA_KV_RANK + LANES), keep_kv),
        (lambda: _dot(low_rank["cq"], wqb_ref[...]),
         roped(mq_ref, 1, MLA_ROPE // 2, LOG2E * (MLA_NOPE + MLA_ROPE) ** -0.5)),
        (lambda: proj(_C_RV, 512), cast_store(rv_ref)),
        (lambda: _dot(low_rank["ckv"], wkbk_ref[...]), key_store),
        (lambda: proj(_C_RG, 512), silu_store),
        (lambda: _dot(low_rank["ckv"], wkbv_ref[...]), cast_store(mv_ref)),
        (lambda: proj(_C_DQ, 512), roped(dq_ref, 2, DIFF_ROT // 2, LOG2E * DIFF_HD ** -0.5)),
        (lambda: proj(_C_DK, 512), roped(dk_ref, 2, DIFF_ROT // 2, 1.0)),
        (lambda: proj(_C_DV, 512), cast_store(dv_ref)),
    )
    _skewed(len(items), lambda n: items[n][0](), lambda n, val: items[n][1](val))


def _inproj(x, gains, wa, q_norm, kv_norm, wqb, wkbk, wkbv, cos_tab, sin_tab, layer):
    t = x.shape[0]
    tm = TOKEN_TILE
    widths = (256, 256, 512, 512, 1024, 1024, 512, 512, 512, 512)

    return pl.pallas_call(
        _inproj_kernel,
        grid=(t // tm,),
        in_specs=[pl.BlockSpec((tm, D_MODEL), lambda i: (i, 0)),
                  _resident((2, D_MODEL), layer),
                  _resident((D_MODEL, PROJ_WIDTH), layer),
                  _resident((1, MLA_Q_RANK), layer),
                  _resident((1, MLA_KV_RANK), layer),
                  _resident((MLA_Q_RANK, MLA_HEADS * LANES), layer),
                  _resident((MLA_KV_RANK, MLA_HEADS * LANES), layer),
                  _resident((MLA_KV_RANK, MLA_HEADS * MLA_DV), layer),
                  pl.BlockSpec((3, tm, LANES), lambda i: (0, i, 0)),
                  pl.BlockSpec((3, tm, LANES), lambda i: (0, i, 0))],
        out_specs=[pl.BlockSpec((tm, w), lambda i: (i, 0)) for w in widths],
        out_shape=[jax.ShapeDtypeStruct((t, w), BF16) for w in widths],
        compiler_params=_params("parallel"),
        name="inproj",
    )(x, gains, wa, q_norm, kv_norm, wqb, wkbk, wkbv, cos_tab, sin_tab)


def _retention_kernel(q_ref, k_ref, v_ref, g_ref, o_ref, state_sc):
    r = RET_TILE

    @pl.when(pl.program_id(1) == 0)
    def _():
        state_sc[...] = jnp.zeros_like(state_sc)

    row = lax.broadcasted_iota(jnp.int32, (r, r), 0)
    col = lax.broadcasted_iota(jnp.int32, (r, r), 1)
    allowed = (col >> 6) <= (row >> 6)
    dist = jnp.abs(row - col).astype(F32)
    lane = lax.broadcasted_iota(jnp.int32, (r, LANES), 1)
    first_head = lane < RET_DK
    n_local = lax.broadcasted_iota(jnp.int32, (r, LANES), 0).astype(F32)
    s_row = lax.broadcasted_iota(jnp.int32, (LANES, 2 * RET_DV), 0)
    s_col = lax.broadcasted_iota(jnp.int32, (LANES, 2 * RET_DV), 1)
    same_head = (s_row < RET_DK) == (s_col < RET_DV)

    for p in range(RET_HEADS // 2):
        lg0, lg1 = RET_LOG_GAMMA[2 * p], RET_LOG_GAMMA[2 * p + 1]
        q = q_ref[:, p * LANES:(p + 1) * LANES]
        k = k_ref[:, p * LANES:(p + 1) * LANES]
        v = v_ref[:, 2 * p * RET_DV:2 * (p + 1) * RET_DV]
        lg_lane = jnp.where(first_head, lg0, lg1)
        state = state_sc[p]

        q_dec = (q.astype(F32) * jnp.exp(lg_lane * (n_local + 1.0))).astype(BF16)
        o_cross = _dot(q_dec, state.astype(BF16))

        for hh in range(2):
            lg = lg1 if hh else lg0
            qm = jnp.where(first_head != bool(hh), q, jnp.zeros_like(q))
            decay = jnp.where(allowed, jnp.exp(lg * dist), 0.0)
            scores = (_dot_nt(qm, k) * decay).astype(BF16)
            sl = slice(hh * RET_DV, (hh + 1) * RET_DV)
            o = _dot(scores, v[:, sl]) + o_cross[:, sl]
            gsl = slice((2 * p + hh) * RET_DV, (2 * p + hh + 1) * RET_DV)
            o_ref[:, gsl] = (_rms(o) * g_ref[:, gsl].astype(F32)).astype(BF16)

        k_dec = k.astype(F32) * jnp.exp(lg_lane * (r - 1.0 - n_local))
        kv = _dot(k_dec.T.astype(BF16), v)
        block_decay = jnp.where(s_col < RET_DV, math.exp(lg0 * r), math.exp(lg1 * r))
        state_sc[p] = state * block_decay + jnp.where(same_head, kv, 0.0)


def _retention(q, k, v, g):
    b, s, _ = q.shape
    r = RET_TILE

    def spec(w):
        return pl.BlockSpec((None, r, w), lambda i, j: (i, j, 0))

    return pl.pallas_call(
        _retention_kernel,
        grid=(b, s // r),
        in_specs=[spec(256), spec(256), spec(512), spec(512)],
        out_specs=spec(512),
        out_shape=jax.ShapeDtypeStruct((b, s, BRANCH_WIDTH), BF16),
        scratch_shapes=[pltpu.VMEM((RET_HEADS // 2, LANES, 2 * RET_DV), F32)],
        compiler_params=_params("parallel", "arbitrary"),
        name="retention",
    )(q, k, v, g)


ONES_ROWS = 16


def _scores_t(q, k_ref, lo, bias_t):
    tq = q.shape[0]
    s_diag = _dot_nt(k_ref[lo:lo + tq, :], q) + bias_t
    m = jnp.max(s_diag, axis=0, keepdims=True)
    s_past = None
    if lo > 0:
        s_past = _dot_nt(k_ref[0:lo, :], q)
        m = jnp.maximum(m, jnp.max(s_past, axis=0, keepdims=True))
    return s_diag, s_past, m


def _weighted_values_t(scores, vt_ref, lo):
    s_diag, s_past, m = scores
    tq = s_diag.shape[1]
    out = _dot(vt_ref[:, lo:lo + tq], jnp.exp2(s_diag - m).astype(BF16))
    for t in range(lo // tq):
        keys = slice(t * tq, (t + 1) * tq)
        out = out + _dot(vt_ref[:, keys], jnp.exp2(s_past[keys, :] - m).astype(BF16))
    return out


def _skewed(n_items, first_stage, second_stage):
    pending = first_stage(0)
    for n in range(n_items):
        ahead = first_stage(n + 1) if n + 1 < n_items else None
        second_stage(n, pending)
        pending = ahead


def _chunk_bias_t(n):
    key = lax.broadcasted_iota(jnp.int32, (n, n), 0)
    qry = lax.broadcasted_iota(jnp.int32, (n, n), 1)
    return jnp.where((key >> 6) <= (qry >> 6), 0.0, NEG_INF).astype(F32)


def _mla_attn_kernel(q_ref, k_ref, v_ref, o_ref, vt_sc):
    tq = ATTN_TILE
    s = q_ref.shape[0]
    vt = v_ref[...].astype(F32).T
    for hh in range(2):
        vt_sc[hh, 0:MLA_DV, :] = vt[hh * MLA_DV:(hh + 1) * MLA_DV, :].astype(BF16)
        vt_sc[hh, MLA_DV:, :] = jnp.ones((ONES_ROWS, s), BF16)
    bias_t = _chunk_bias_t(tq)

    def scores(i):
        lo = i * tq
        return [_scores_t(q_ref[lo:lo + tq, hh * LANES:(hh + 1) * LANES],
                          k_ref.at[:, hh * LANES:(hh + 1) * LANES], lo, bias_t) for hh in range(2)]

    def outputs(i, sc):
        lo = i * tq
        accs = [_weighted_values_t(sc[hh], vt_sc.at[hh], lo) for hh in range(2)]
        halves = [acc[0:MLA_DV, :] / acc[MLA_DV:MLA_DV + 1, :] for acc in accs]
        o_ref[lo:lo + tq, :] = jnp.concatenate(halves, axis=0).T.astype(BF16)

    _skewed(s // tq, scores, outputs)


def _mla_attention(q, k, v):
    b, s, _ = q.shape
    pairs = MLA_HEADS // 2
    return pl.pallas_call(
        _mla_attn_kernel,
        grid=(b, pairs),
        in_specs=[pl.BlockSpec((None, s, 2 * LANES), lambda i, j: (i, 0, j)),
                  pl.BlockSpec((None, s, 2 * LANES), lambda i, j: (i, 0, j)),
                  pl.BlockSpec((None, s, LANES), lambda i, j: (i, 0, j))],
        out_specs=pl.BlockSpec((None, s, LANES), lambda i, j: (i, 0, j)),
        out_shape=jax.ShapeDtypeStruct((b, s, BRANCH_WIDTH), BF16),
        scratch_shapes=[pltpu.VMEM((2, MLA_DV + ONES_ROWS, s), BF16)],
        compiler_params=_params("parallel", "parallel"),
        name="mla_attention",
    )(q, k, v)


def _diff_attn_kernel(q_ref, k_ref, v_ref, lam_ref, linit_ref, o_ref, vt_sc):
    tq = ATTN_TILE
    s = q_ref.shape[0]
    dv = 2 * DIFF_HD
    vt_sc[0:dv, :] = v_ref[...].astype(F32).T.astype(BF16)
    vt_sc[dv:, :] = jnp.ones((ONES_ROWS, s), BF16)
    bias_t = _chunk_bias_t(tq)
    lane = lax.broadcasted_iota(jnp.int32, (tq, LANES), 1)
    lp = lam_ref[...]
    linit = linit_ref[...]
    lam = (jnp.exp(jnp.sum(lp[0:1] * lp[1:2], axis=-1, keepdims=True))
           - jnp.exp(jnp.sum(lp[2:3] * lp[3:4], axis=-1, keepdims=True)) + linit)
    def scores(i):
        lo = i * tq
        q = q_ref[lo:lo + tq, :]
        zero = jnp.zeros_like(q)
        return [_scores_t(jnp.where((lane < DIFF_HD) != bool(j), q, zero), k_ref, lo, bias_t)
                for j in range(2)]

    def outputs(i, sc):
        lo = i * tq
        accs = [_weighted_values_t(sc[j], vt_sc, lo) for j in range(2)]
        maps = [acc[0:dv, :] / acc[dv:dv + 1, :] for acc in accs]
        o_t = maps[0] - lam * maps[1]
        o_t = o_t * lax.rsqrt(jnp.mean(o_t * o_t, axis=0, keepdims=True) + EPS)
        o_ref[lo:lo + tq, :] = (o_t * (1.0 - linit)).T.astype(BF16)

    _skewed(s // tq, scores, outputs)


def _diff_attention(q, k, v, lam_params, lambda_init):
    b, s, _ = q.shape

    def spec():
        return pl.BlockSpec((None, s, LANES), lambda i, j: (i, 0, j))

    return pl.pallas_call(
        _diff_attn_kernel,
        grid=(b, DIFF_HEADS),
        in_specs=[spec(), spec(), spec(),
                  pl.BlockSpec((4, DIFF_HD), lambda i, j: (0, 0)),
                  pl.BlockSpec((1, 1), lambda i, j: (0, 0))],
        out_specs=spec(),
        out_shape=jax.ShapeDtypeStruct((b, s, BRANCH_WIDTH), BF16),
        scratch_shapes=[pltpu.VMEM((2 * DIFF_HD + ONES_ROWS, s), BF16)],
        compiler_params=_params("parallel", "parallel"),
        name="diff_attention",
    )(q, k, v, lam_params, jnp.full((1, 1), lambda_init, F32))


def _merge_kernel(x_ref, g_ref, yr_ref, ym_ref, yd_ref, wg_ref, wb_ref, wo_ref, o_ref):
    def pre_norm(n, _):
        return _rms(x_ref[_row_block(n), :], g_ref[0:1, :]).astype(BF16)

    def gated_sum(n, h):
        mixed = None
        for br, y_ref in enumerate((yr_ref, ym_ref, yd_ref)):
            gate = jax.nn.sigmoid(_dot(h, wg_ref[:, br * D_MODEL:(br + 1) * D_MODEL]))
            term = gate * _dot(y_ref[_row_block(n), :], wb_ref[br])
            mixed = term if mixed is None else mixed + term
        return mixed.astype(BF16)

    def project(n, mixed):
        y = _dot(mixed, wo_ref[...])
        o_ref[_row_block(n), :] = x_ref[_row_block(n), :] + _rms(y, g_ref[1:2, :])

    _wavefront(x_ref.shape[0] // SUB_TILE, (pre_norm, gated_sum, project))


def _merge(x, gains, y_ret, y_mla, y_diff, wg, wb, wo, layer):
    t = x.shape[0]
    tm = DENSE_TILE

    def rows(w):
        return pl.BlockSpec((tm, w), lambda i: (i, 0))

    return pl.pallas_call(
        _merge_kernel,
        grid=(t // tm,),
        in_specs=[rows(D_MODEL),
                  _resident((2, D_MODEL), layer),
                  rows(BRANCH_WIDTH), rows(BRANCH_WIDTH), rows(BRANCH_WIDTH),
                  _resident((D_MODEL, N_BRANCH * D_MODEL), layer),
                  _resident((N_BRANCH, BRANCH_WIDTH, D_MODEL), layer),
                  _resident((D_MODEL, D_MODEL), layer)],
        out_specs=rows(D_MODEL),
        out_shape=jax.ShapeDtypeStruct((t, D_MODEL), F32),
        compiler_params=_params("parallel"),
        name="merge",
    )(x, gains, y_ret, y_mla, y_diff, wg, wb, wo)


def _memkv_kernel(mem_ref, g_ref, w_ref, o_ref):
    m = _rms(mem_ref[...], g_ref[2:3, :]).astype(BF16)
    o_ref[...] = _dot(m, w_ref[...]).astype(BF16)


def _memkv(mem, gains, wkv):
    b, n, _ = mem.shape
    depth = wkv.shape[0]
    return pl.pallas_call(
        _memkv_kernel,
        grid=(depth, b),
        in_specs=[pl.BlockSpec((None, n, D_MODEL), lambda l, i: (i, 0, 0)),
                  pl.BlockSpec((None, 3, D_MODEL), lambda l, i: (l, 0, 0)),
                  pl.BlockSpec((None, D_MODEL, 2 * D_MODEL), lambda l, i: (l, 0, 0))],
        out_specs=pl.BlockSpec((None, None, n, 2 * D_MODEL), lambda l, i: (l, i, 0, 0)),
        out_shape=jax.ShapeDtypeStruct((depth, b, n, 2 * D_MODEL), BF16),
        compiler_params=_params("parallel", "parallel"),
        name="memkv",
    )(mem, gains, wkv)


def _cross_kernel(x_ref, g_ref, kv_ref, wq_ref, wo_ref, o_ref):
    def query(n, _):
        h = _rms(x_ref[_row_block(n), :], g_ref[0:1, :]).astype(BF16)
        return (_dot(h, wq_ref[...]) * (LOG2E * CROSS_HD ** -0.5)).astype(BF16)

    def attend(n, q):
        heads = []
        for hd in range(CROSS_HEADS):
            cols = slice(hd * CROSS_HD, (hd + 1) * CROSS_HD)
            vcols = slice(D_MODEL + hd * CROSS_HD, D_MODEL + (hd + 1) * CROSS_HD)
            sc = _dot_nt(q[:, cols], kv_ref[:, cols])
            p = jnp.exp2(sc - jnp.max(sc, axis=-1, keepdims=True))
            total = jnp.sum(p, axis=-1, keepdims=True)
            heads.append((_dot(p.astype(BF16), kv_ref[:, vcols]) / total).astype(BF16))
        return jnp.concatenate(heads, axis=-1)

    def project(n, o):
        y = _dot(o, wo_ref[...])
        o_ref[_row_block(n), :] = x_ref[_row_block(n), :] + _rms(y, g_ref[1:2, :])

    _wavefront(x_ref.shape[0] // SUB_TILE, (query, attend, project))


def _cross(x, gains, kv, wq, wo, layer):
    b, s, _ = x.shape
    tm = DENSE_TILE
    n = kv.shape[2]
    return pl.pallas_call(
        _cross_kernel,
        grid=(b, s // tm),
        in_specs=[pl.BlockSpec((None, tm, D_MODEL), lambda i, j: (i, j, 0)),
                  _resident((3, D_MODEL), layer),
                  pl.BlockSpec((None, None, n, 2 * D_MODEL), lambda i, j: (layer, i, 0, 0)),
                  _resident((D_MODEL, D_MODEL), layer),
                  _resident((D_MODEL, D_MODEL), layer)],
        out_specs=pl.BlockSpec((None, tm, D_MODEL), lambda i, j: (i, j, 0)),
        out_shape=jax.ShapeDtypeStruct((b, s, D_MODEL), F32),
        compiler_params=_params("parallel", "parallel"),
        name="cross_attention",
    )(x, gains, kv, wq, wo)


def _pack_inproj(w_in, wq_b, wkv_b):
    depth, d, _ = w_in.shape
    k_rope_at = 1920
    def placed(first, last, at):
        return jnp.pad(w_in[:, :, first:last].astype(BF16),
                       ((0, 0), (0, 0), (at, PROJ_WIDTH - at - (last - first))))

    wa = (placed(0, k_rope_at, 0)
          + placed(k_rope_at, k_rope_at + MLA_ROPE, _C_MKV + MLA_KV_RANK + MLA_NOPE)
          + placed(k_rope_at + MLA_ROPE, GATE_OFFSET, _C_DQ))
    wg = w_in[:, :, GATE_OFFSET:].astype(BF16)
    wq = wq_b.astype(BF16).reshape(depth, MLA_Q_RANK, MLA_HEADS, MLA_NOPE + MLA_ROPE)
    wq = jnp.pad(wq, ((0, 0), (0, 0), (0, 0), (0, LANES - MLA_NOPE - MLA_ROPE)))
    wq = wq.reshape(depth, MLA_Q_RANK, MLA_HEADS * LANES)
    wkv = wkv_b.astype(BF16).reshape(depth, MLA_KV_RANK, MLA_HEADS, MLA_NOPE + MLA_DV)
    wk = jnp.pad(wkv[..., :MLA_NOPE], ((0, 0), (0, 0), (0, 0), (0, LANES - MLA_NOPE)))
    wk = wk.reshape(depth, MLA_KV_RANK, MLA_HEADS * LANES)
    wv = wkv[..., MLA_NOPE:].reshape(depth, MLA_KV_RANK, MLA_HEADS * MLA_DV)
    return wa, wg, wq, wk, wv


def kernel(x, mem, positions, ffn1_norms, ffn1_w13, ffn1_w2, mix_norms, w_in, mla_q_norm,
           mla_kv_norm, mla_wq_b, mla_wkv_b, diff_lambda, w_branch, w_out, cross_norms,
           cross_wq, cross_wkv, cross_wo, ffn2_norms, ffn2_w13, ffn2_w2):
    b, s, d = x.shape
    t = b * s
    depth = w_in.shape[0]
    cos_tab, sin_tab = _rope_tables(positions)

    f1_w13, f1_w2 = ffn1_w13.astype(BF16), ffn1_w2.astype(BF16)
    f2_w13, f2_w2 = ffn2_w13.astype(BF16), ffn2_w2.astype(BF16)
    wa, wg, wqb, wkbk, wkbv = _pack_inproj(w_in, mla_wq_b, mla_wkv_b)
    wb, wo = w_branch.astype(BF16), w_out.astype(BF16)
    c_wq, c_wo = cross_wq.astype(BF16), cross_wo.astype(BF16)
    q_norm = mla_q_norm.reshape(depth, 1, MLA_Q_RANK)
    kv_norm = mla_kv_norm.reshape(depth, 1, MLA_KV_RANK)

    mem_kv = _memkv(mem, cross_norms, cross_wkv.astype(BF16))

    def seq(a):
        return a.reshape(b, s, a.shape[-1])

    xf = x.reshape(t, d)
    for l in range(depth):
        lambda_init = 0.8 - 0.6 * math.exp(-0.3 * l)
        xf = _ffn(xf, ffn1_norms, f1_w13, f1_w2, l)

        (rq, rk, rv, rg, mq, mk, mv, dq, dk, dv) = _inproj(
            xf, mix_norms, wa, q_norm, kv_norm, wqb, wkbk, wkbv, cos_tab, sin_tab, l)
        y_ret = _retention(seq(rq), seq(rk), seq(rv), seq(rg))
        y_mla = _mla_attention(seq(mq), seq(mk), seq(mv))
        y_diff = _diff_attention(seq(dq), seq(dk), seq(dv), diff_lambda[l], lambda_init)
        xf = _merge(xf, mix_norms, y_ret.reshape(t, -1), y_mla.reshape(t, -1),
                    y_diff.reshape(t, -1), wg, wb, wo, l)

        xf = _cross(xf.reshape(b, s, d), cross_norms, mem_kv, c_wq, c_wo, l).reshape(t, d)

        xf = _ffn(xf, ffn2_norms, f2_w13, f2_w2, l)
    return xf.reshape(b, s, d)
```

```python
import functools
import math

import numpy as np
import jax
import jax.numpy as jnp
from jax import lax
from jax.experimental import pallas as pl
from jax.experimental.pallas import tpu as pltpu

F32 = jnp.float32
BF16 = jnp.bfloat16

D_MODEL = 1024
DEPTH = 4
CHUNK = 64
EPS = 1e-6
NEG_INF = -1e30
ROPE_THETA = 500000.0
RET_THETA = 10000.0

RET_HEADS = 4
RET_DK = 64
RET_DV = 128
MLA_HEADS = 8
MLA_Q_RANK = 256
MLA_KV_RANK = 128
MLA_NOPE = 64
MLA_ROPE = 32
MLA_DV = 64
DIFF_HEADS = 4
DIFF_HD = 64
DIFF_ROT = 16
N_BRANCH = 3
BRANCH_WIDTH = 512
CROSS_HEADS = 4
CROSS_HD = 256
D_FF = 2816
GATE_OFFSET = 3488

LANES = 128
VMEM_LIMIT = 56 * 1024 * 1024

TOKEN_TILE = 512
DENSE_TILE = 1024
SUB_TILE = 256
ATTN_TILE = 256
RET_TILE = 256

LOG2E = math.log2(math.e)
RET_LOG_GAMMA =tuple(math.log(1.0 - 2.0 ** (-5.0 - h)) for h in range(RET_HEADS))


def _params(*sem):
    return pltpu.CompilerParams(dimension_semantics=sem, vmem_limit_bytes=VMEM_LIMIT)


def _rms(x, gain=None):
    y = x * lax.rsqrt(jnp.mean(x * x, axis=-1, keepdims=True) + EPS)
    return y if gain is None else y * gain


def _dot(a, b):
    return jnp.dot(a, b, preferred_element_type=F32)


def _dot_nt(a, b):
    return lax.dot_general(a, b, (((1,), (1,)), ((), ())), preferred_element_type=F32)


def _rope(x, cos, sin_signed, half):
    lane = lax.broadcasted_iota(jnp.int32, x.shape, 1)
    first = (lane & (2 * half - 1)) < half
    partner = jnp.where(first, pltpu.roll(x, LANES - half, 1), pltpu.roll(x, half, 1))
    return x * cos + partner * sin_signed


def _rope_patterns():
    inv = np.zeros((1, LANES), np.float32)
    inv[0, 0:32] = 1.0 / (RET_THETA ** (np.arange(0, RET_DK, 2, dtype=np.float32) / RET_DK))
    inv[0, 32:48] = 1.0 / (ROPE_THETA ** (np.arange(0, MLA_ROPE, 2, dtype=np.float32) / MLA_ROPE))
    inv[0, 48:56] = 1.0 / (ROPE_THETA ** (np.arange(0, DIFF_ROT, 2, dtype=np.float32) / DIFF_ROT))
    shift = np.full((3, LANES), -1, np.int32)
    sign = np.zeros((3, LANES), np.float32)
    for i in range(LANES):
        j = i % RET_DK
        shift[0, i] = (i - j % 32) % LANES
        sign[0, i] = -1.0 if j < 32 else 1.0
        if 64 <= i < 96:
            j = i - 64
            shift[1, i] = (i - (32 + j % 16)) % LANES
            sign[1, i] = -1.0 if j < 16 else 1.0
        j = i % DIFF_HD
        if j < DIFF_ROT:
            shift[2, i] = (i - (48 + j % 8)) % LANES
            sign[2, i] = -1.0 if j < 8 else 1.0
    return inv, shift, sign


_ROPE_INV, _ROPE_SHIFT, _ROPE_SIGN = _rope_patterns()


def _rope_table_kernel(pos_ref, inv_ref, shift_ref, sign_ref, cos_ref, sin_ref):
    ang = pos_ref[...].astype(F32) * inv_ref[...]
    packed = {"cos": jnp.cos(ang), "sin": jnp.sin(ang)}
    rolled = {}

    def moved(name, k):
        if k == 0:
            return packed[name]
        if (name, k) not in rolled:
            rolled[name, k] = pltpu.roll(packed[name], k, 1)
        return rolled[name, k]

    for t in range(3):
        lane_shift = shift_ref[t:t + 1, :]
        cos_t = jnp.ones_like(ang)
        sin_t = jnp.zeros_like(ang)
        for k in sorted(set(int(v) for v in _ROPE_SHIFT[t] if v >= 0)):
            cos_t = jnp.where(lane_shift == k, moved("cos", k), cos_t)
            sin_t = jnp.where(lane_shift == k, moved("sin", k), sin_t)
        cos_ref[t] = cos_t
        sin_ref[t] = sin_t * sign_ref[t:t + 1, :]


def _rope_tables(positions):
    t = positions.size
    tm = TOKEN_TILE
    tab = jax.ShapeDtypeStruct((3, t, LANES), F32)
    return pl.pallas_call(
        _rope_table_kernel,
        grid=(t // tm,),
        in_specs=[pl.BlockSpec((tm, 1), lambda i: (i, 0)),
                  pl.BlockSpec((1, LANES), lambda i: (0, 0)),
                  pl.BlockSpec((3, LANES), lambda i: (0, 0)),
                  pl.BlockSpec((3, LANES), lambda i: (0, 0))],
        out_specs=[pl.BlockSpec((3, tm, LANES), lambda i: (0, i, 0)),
                   pl.BlockSpec((3, tm, LANES), lambda i: (0, i, 0))],
        out_shape=[tab, tab],
        compiler_params=_params("parallel"),
        name="rope_tables",
    )(positions.reshape(t, 1), jnp.asarray(_ROPE_INV), jnp.asarray(_ROPE_SHIFT),
      jnp.asarray(_ROPE_SIGN))


def _wavefront(n_items, stages):
    state = [None] * n_items
    for wave in range(n_items + len(stages) - 1):
        for s in reversed(range(len(stages))):
            n = wave - s
            if 0 <= n < n_items:
                state[n] = stages[s](n, state[n])


def _row_block(n):
    return pl.ds(n * SUB_TILE, SUB_TILE)


def _ffn_kernel(x_ref, g_ref, w13_ref, w2_ref, o_ref):
    def pre_norm(n, _):
        return _rms(x_ref[_row_block(n), :], g_ref[0:1, :]).astype(BF16)

    def hidden(n, h):
        gate = _dot(h, w13_ref[:, :D_FF])
        up = _dot(h, w13_ref[:, D_FF:])
        return (gate * jax.nn.sigmoid(gate) * up).astype(BF16)

    def project(n, act):
        y = _dot(act, w2_ref[...])
        o_ref[_row_block(n), :] = x_ref[_row_block(n), :] + 0.5 * _rms(y, g_ref[1:2, :])

    _wavefront(x_ref.shape[0] // SUB_TILE, (pre_norm, hidden, project))


def _resident(shape, layer):
    zeros = (0,) * len(shape)
    return pl.BlockSpec((None,) + tuple(shape), lambda *_: (layer,) + zeros,
                        pipeline_mode=pl.Buffered(1))


def _ffn(x, gains, w13, w2, layer):
    t = x.shape[0]
    tm = DENSE_TILE
    return pl.pallas_call(
        _ffn_kernel,
        grid=(t // tm,),
        in_specs=[pl.BlockSpec((tm, D_MODEL), lambda i: (i, 0)),
                  _resident((2, D_MODEL), layer),
                  _resident((D_MODEL, 2 * D_FF), layer),
                  _resident((D_FF, D_MODEL), layer)],
        out_specs=pl.BlockSpec((tm, D_MODEL), lambda i: (i, 0)),
        out_shape=jax.ShapeDtypeStruct((t, D_MODEL), F32),
        compiler_params=_params("parallel"),
        name="ffn",
    )(x, gains, w13, w2)


_C_RQ, _C_RK, _C_RV, _C_RG = 0, 256, 512, 1024
_C_MQ, _C_MKV = 1536, 1792
_C_DQ, _C_DK, _C_DV = 2048, 2560, 3072
PROJ_WIDTH = 3584


def _inproj_kernel(x_ref, g_ref, wa_ref, qn_ref, kvn_ref, wqb_ref, wkbk_ref, wkbv_ref,
                   cos_ref, sin_ref,
                   rq_ref, rk_ref, rv_ref, rg_ref, mq_ref, mk_ref, mv_ref,
                   dq_ref, dk_ref, dv_ref):
    h = _rms(x_ref[...], g_ref[0:1, :]).astype(BF16)

    def proj(start, width):
        return _dot(h, wa_ref[:, start:start + width])

    def rope_store(dst, val, table, half, scale):
        cos, sin = cos_ref[table], sin_ref[table]
        for c in range(val.shape[1] // LANES):
            sl = slice(c * LANES, (c + 1) * LANES)
            r = _rope(val[:, sl], cos, sin, half)
            if scale != 1.0:
                r = r * scale
            dst[:, sl] = r.astype(BF16)

    def cast_store(dst):
        def store(val):
            dst[...] = val.astype(BF16)
        return store

    def silu_store(val):
        rg_ref[...] = (val * jax.nn.sigmoid(val)).astype(BF16)

    low_rank = {}

    def keep_cq(val):
        low_rank["cq"] = _rms(val, qn_ref[...]).astype(BF16)

    def keep_kv(val):
        low_rank["ckv"] = _rms(val[:, :MLA_KV_RANK], kvn_ref[...]).astype(BF16)
        low_rank["k_rope"] = _rope(val[:, MLA_KV_RANK:], cos_ref[1], sin_ref[1], MLA_ROPE // 2)

    def key_store(k_nope):
        for hd in range(MLA_HEADS):
            sl = slice(hd * LANES, (hd + 1) * LANES)
            mk_ref[:, sl] = (k_nope[:, sl] + low_rank["k_rope"]).astype(BF16)

    def roped(dst, table, half, scale):
        return lambda val: rope_store(dst, val, table, half, scale)

    items = (
        (lambda: proj(_C_RQ, 256), roped(rq_ref, 0, RET_DK // 2, RET_DK ** -0.5)),
        (lambda: proj(_C_MQ, MLA_Q_RANK), keep_cq),
        (lambda: proj(_C_RK, 256), roped(rk_ref, 0, RET_DK // 2, 1.0)),
        (lambda: proj(_C_MKV, MLA_KV_RANK + LANES), keep_kv),
        (lambda: _dot(low_rank["cq"], wqb_ref[...]),
         roped(mq_ref, 1, MLA_ROPE // 2, LOG2E * (MLA_NOPE + MLA_ROPE) ** -0.5)),
        (lambda: proj(_C_RV, 512), cast_store(rv_ref)),
        (lambda: _dot(low_rank["ckv"], wkbk_ref[...]), key_store),
        (lambda: proj(_C_RG, 512), silu_store),
        (lambda: _dot(low_rank["ckv"], wkbv_ref[...]), cast_store(mv_ref)),
        (lambda: proj(_C_DQ, 512), roped(dq_ref, 2, DIFF_ROT // 2, LOG2E * DIFF_HD ** -0.5)),
        (lambda: proj(_C_DK, 512), roped(dk_ref, 2, DIFF_ROT // 2, 1.0)),
        (lambda: proj(_C_DV, 512), cast_store(dv_ref)),
    )
    _skewed(len(items), lambda n: items[n][0](), lambda n, val: items[n][1](val))


def _inproj(x, gains, wa, q_norm, kv_norm, wqb, wkbk, wkbv, cos_tab, sin_tab, layer):
    t = x.shape[0]
    tm = TOKEN_TILE
    widths = (256, 256, 512, 512, 1024, 1024, 512, 512, 512, 512)

    return pl.pallas_call(
        _inproj_kernel,
        grid=(t // tm,),
        in_specs=[pl.BlockSpec((tm, D_MODEL), lambda i: (i, 0)),
                  _resident((2, D_MODEL), layer),
                  _resident((D_MODEL, PROJ_WIDTH), layer),
                  _resident((1, MLA_Q_RANK), layer),
                  _resident((1, MLA_KV_RANK), layer),
                  _resident((MLA_Q_RANK, MLA_HEADS * LANES), layer),
                  _resident((MLA_KV_RANK, MLA_HEADS * LANES), layer),
                  _resident((MLA_KV_RANK, MLA_HEADS * MLA_DV), layer),
                  pl.BlockSpec((3, tm, LANES), lambda i: (0, i, 0)),
                  pl.BlockSpec((3, tm, LANES), lambda i: (0, i, 0))],
        out_specs=[pl.BlockSpec((tm, w), lambda i: (i, 0)) for w in widths],
        out_shape=[jax.ShapeDtypeStruct((t, w), BF16) for w in widths],
        compiler_params=_params("parallel"),
        name="inproj",
    )(x, gains, wa, q_norm, kv_norm, wqb, wkbk, wkbv, cos_tab, sin_tab)


def _retention_kernel(q_ref, k_ref, v_ref, g_ref, o_ref, state_sc, decay_sc, qk_dec_sc):
    r = RET_TILE
    state_sc[...] = jnp.zeros_like(state_sc)

    row = lax.broadcasted_iota(jnp.int32, (r, r), 0)
    col = lax.broadcasted_iota(jnp.int32, (r, r), 1)
    allowed = (col >> 6) <= (row >> 6)
    dist = jnp.abs(row - col).astype(F32)
    for h in range(RET_HEADS):
        decay_sc[h] = jnp.where(allowed, jnp.exp(RET_LOG_GAMMA[h] * dist), 0.0)
    lane = lax.broadcasted_iota(jnp.int32, (r, LANES), 1)
    first_head = lane < RET_DK
    n_local = lax.broadcasted_iota(jnp.int32, (r, LANES), 0).astype(F32)
    for p in range(RET_HEADS // 2):
        lg_lane = jnp.where(first_head, RET_LOG_GAMMA[2 * p], RET_LOG_GAMMA[2 * p + 1])
        qk_dec_sc[p, 0] = jnp.exp(lg_lane * (n_local + 1.0))
        qk_dec_sc[p, 1] = jnp.exp(lg_lane * (r - 1.0 - n_local))
    s_row = lax.broadcasted_iota(jnp.int32, (LANES, 2 * RET_DV), 0)
    s_col = lax.broadcasted_iota(jnp.int32, (LANES, 2 * RET_DV), 1)
    same_head = (s_row < RET_DK) == (s_col < RET_DV)

    def block(t, carry):
        rows = pl.ds(pl.multiple_of(t * r, r), r)
        for p in range(RET_HEADS // 2):
            lg0, lg1 = RET_LOG_GAMMA[2 * p], RET_LOG_GAMMA[2 * p + 1]
            q = q_ref[rows, p * LANES:(p + 1) * LANES]
            k = k_ref[rows, p * LANES:(p + 1) * LANES]
            v = v_ref[rows, 2 * p * RET_DV:2 * (p + 1) * RET_DV]
            state = state_sc[p]

            q_dec = (q.astype(F32) * qk_dec_sc[p, 0]).astype(BF16)
            o_cross = _dot(q_dec, state.astype(BF16))

            for hh in range(2):
                qm = jnp.where(first_head != bool(hh), q, jnp.zeros_like(q))
                scores = (_dot_nt(qm, k) * decay_sc[2 * p + hh]).astype(BF16)
                sl = slice(hh * RET_DV, (hh + 1) * RET_DV)
                o = _dot(scores, v[:, sl]) + o_cross[:, sl]
                gsl = slice((2 * p + hh) * RET_DV, (2 * p + hh + 1) * RET_DV)
                o_ref[rows, gsl] = (_rms(o) * g_ref[rows, gsl].astype(F32)).astype(BF16)

            k_dec = k.astype(F32) * qk_dec_sc[p, 1]
            kv = _dot(k_dec.T.astype(BF16), v)
            block_decay = jnp.where(s_col < RET_DV, math.exp(lg0 * r), math.exp(lg1 * r))
            state_sc[p] = state * block_decay + jnp.where(same_head, kv, 0.0)
        return carry

    lax.fori_loop(0, q_ref.shape[0] // r, block, 0)


def _retention(q, k, v, g):
    b, s, _ = q.shape
    r = RET_TILE

    def spec(w):
        return pl.BlockSpec((None, s, w), lambda i: (i, 0, 0))

    return pl.pallas_call(
        _retention_kernel,
        grid=(b,),
        in_specs=[spec(256), spec(256), spec(512), spec(512)],
        out_specs=spec(512),
        out_shape=jax.ShapeDtypeStruct((b, s, BRANCH_WIDTH), BF16),
        scratch_shapes=[pltpu.VMEM((RET_HEADS // 2, LANES, 2 * RET_DV), F32),
                        pltpu.VMEM((RET_HEADS, r, r), F32),
                        pltpu.VMEM((RET_HEADS // 2, 2, r, LANES), F32)],
        compiler_params=_params("parallel"),
        name="retention",
    )(q, k, v, g)


ONES_ROWS = 16


def _scores_t(q, k_ref, lo, bias_t):
    tq = q.shape[0]
    s_diag = _dot_nt(k_ref[lo:lo + tq, :], q) + bias_t
    m = jnp.max(s_diag, axis=0, keepdims=True)
    s_past = None
    if lo > 0:
        s_past = _dot_nt(k_ref[0:lo, :], q)
        m = jnp.maximum(m, jnp.max(s_past, axis=0, keepdims=True))
    return s_diag, s_past, m


def _weighted_values_t(scores, vt_ref, lo):
    s_diag, s_past, m = scores
    tq = s_diag.shape[1]
    out = _dot(vt_ref[:, lo:lo + tq], jnp.exp2(s_diag - m).astype(BF16))
    for t in range(lo // tq):
        keys = slice(t * tq, (t + 1) * tq)
        out = out + _dot(vt_ref[:, keys], jnp.exp2(s_past[keys, :] - m).astype(BF16))
    return out


def _skewed(n_items, first_stage, second_stage):
    pending = first_stage(0)
    for n in range(n_items):
        ahead = first_stage(n + 1) if n + 1 < n_items else None
        second_stage(n, pending)
        pending = ahead


def _chunk_bias_t(n):
    key = lax.broadcasted_iota(jnp.int32, (n, n), 0)
    qry = lax.broadcasted_iota(jnp.int32, (n, n), 1)
    return jnp.where((key >> 6) <= (qry >> 6), 0.0, NEG_INF).astype(F32)


def _mla_attn_kernel(q_ref, k_ref, v_ref, o_ref, vt_sc):
    tq = ATTN_TILE
    s = q_ref.shape[0]
    vt = v_ref[...].astype(F32).T
    for hh in range(2):
        vt_sc[hh, 0:MLA_DV, :] = vt[hh * MLA_DV:(hh + 1) * MLA_DV, :].astype(BF16)
        vt_sc[hh, MLA_DV:, :] = jnp.ones((ONES_ROWS, s), BF16)
    bias_t = _chunk_bias_t(tq)

    def scores(i):
        lo = i * tq
        return [_scores_t(q_ref[lo:lo + tq, hh * LANES:(hh + 1) * LANES],
                          k_ref.at[:, hh * LANES:(hh + 1) * LANES], lo, bias_t) for hh in range(2)]

    def outputs(i, sc):
        lo = i * tq
        accs = [_weighted_values_t(sc[hh], vt_sc.at[hh], lo) for hh in range(2)]
        halves = [acc[0:MLA_DV, :] / acc[MLA_DV:MLA_DV + 1, :] for acc in accs]
        o_ref[lo:lo + tq, :] = jnp.concatenate(halves, axis=0).T.astype(BF16)

    _skewed(s // tq, scores, outputs)


def _mla_attention(q, k, v):
    b, s, _ = q.shape
    pairs = MLA_HEADS // 2
    return pl.pallas_call(
        _mla_attn_kernel,
        grid=(b, pairs),
        in_specs=[pl.BlockSpec((None, s, 2 * LANES), lambda i, j: (i, 0, j)),
                  pl.BlockSpec((None, s, 2 * LANES), lambda i, j: (i, 0, j)),
                  pl.BlockSpec((None, s, LANES), lambda i, j: (i, 0, j))],
        out_specs=pl.BlockSpec((None, s, LANES), lambda i, j: (i, 0, j)),
        out_shape=jax.ShapeDtypeStruct((b, s, BRANCH_WIDTH), BF16),
        scratch_shapes=[pltpu.VMEM((2, MLA_DV + ONES_ROWS, s), BF16)],
        compiler_params=_params("parallel", "parallel"),
        name="mla_attention",
    )(q, k, v)


def _diff_attn_kernel(q_ref, k_ref, v_ref, lam_ref, linit_ref, o_ref, vt_sc):
    tq = ATTN_TILE
    s = q_ref.shape[0]
    dv = 2 * DIFF_HD
    vt_sc[0:dv, :] = v_ref[...].astype(F32).T.astype(BF16)
    vt_sc[dv:, :] = jnp.ones((ONES_ROWS, s), BF16)
    bias_t = _chunk_bias_t(tq)
    lane = lax.broadcasted_iota(jnp.int32, (tq, LANES), 1)
    lp = lam_ref[...]
    linit = linit_ref[...]
    lam = (jnp.exp(jnp.sum(lp[0:1] * lp[1:2], axis=-1, keepdims=True))
           - jnp.exp(jnp.sum(lp[2:3] * lp[3:4], axis=-1, keepdims=True)) + linit)
    def scores(i):
        lo = i * tq
        q = q_ref[lo:lo + tq, :]
        zero = jnp.zeros_like(q)
        return [_scores_t(jnp.where((lane < DIFF_HD) != bool(j), q, zero), k_ref, lo, bias_t)
                for j in range(2)]

    def outputs(i, sc):
        lo = i * tq
        accs = [_weighted_values_t(sc[j], vt_sc, lo) for j in range(2)]
        maps = [acc[0:dv, :] / acc[dv:dv + 1, :] for acc in accs]
        o_t = maps[0] - lam * maps[1]
        o_t = o_t * lax.rsqrt(jnp.mean(o_t * o_t, axis=0, keepdims=True) + EPS)
        o_ref[lo:lo + tq, :] = (o_t * (1.0 - linit)).T.astype(BF16)

    _skewed(s // tq, scores, outputs)


def _diff_attention(q, k, v, lam_params, lambda_init):
    b, s, _ = q.shape

    def spec():
        return pl.BlockSpec((None, s, LANES), lambda i, j: (i, 0, j))

    return pl.pallas_call(
        _diff_attn_kernel,
        grid=(b, DIFF_HEADS),
        in_specs=[spec(), spec(), spec(),
                  pl.BlockSpec((4, DIFF_HD), lambda i, j: (0, 0)),
                  pl.BlockSpec((1, 1), lambda i, j: (0, 0))],
        out_specs=spec(),
        out_shape=jax.ShapeDtypeStruct((b, s, BRANCH_WIDTH), BF16),
        scratch_shapes=[pltpu.VMEM((2 * DIFF_HD + ONES_ROWS, s), BF16)],
        compiler_params=_params("parallel", "parallel"),
        name="diff_attention",
    )(q, k, v, lam_params, jnp.full((1, 1), lambda_init, F32))


def _merge_kernel(x_ref, g_ref, yr_ref, ym_ref, yd_ref, wg_ref, wb_ref, wo_ref, o_ref):
    def pre_norm(n, _):
        return _rms(x_ref[_row_block(n), :], g_ref[0:1, :]).astype(BF16)

    def gated_sum(n, h):
        mixed = None
        for br, y_ref in enumerate((yr_ref, ym_ref, yd_ref)):
            gate = jax.nn.sigmoid(_dot(h, wg_ref[:, br * D_MODEL:(br + 1) * D_MODEL]))
            term = gate * _dot(y_ref[_row_block(n), :], wb_ref[br])
            mixed = term if mixed is None else mixed + term
        return mixed.astype(BF16)

    def project(n, mixed):
        y = _dot(mixed, wo_ref[...])
        o_ref[_row_block(n), :] = x_ref[_row_block(n), :] + _rms(y, g_ref[1:2, :])

    _wavefront(x_ref.shape[0] // SUB_TILE, (pre_norm, gated_sum, project))


def _merge(x, gains, y_ret, y_mla, y_diff, wg, wb, wo, layer):
    t = x.shape[0]
    tm = DENSE_TILE

    def rows(w):
        return pl.BlockSpec((tm, w), lambda i: (i, 0))

    return pl.pallas_call(
        _merge_kernel,
        grid=(t // tm,),
        in_specs=[rows(D_MODEL),
                  _resident((2, D_MODEL), layer),
                  rows(BRANCH_WIDTH), rows(BRANCH_WIDTH), rows(BRANCH_WIDTH),
                  _resident((D_MODEL, N_BRANCH * D_MODEL), layer),
                  _resident((N_BRANCH, BRANCH_WIDTH, D_MODEL), layer),
                  _resident((D_MODEL, D_MODEL), layer)],
        out_specs=rows(D_MODEL),
        out_shape=jax.ShapeDtypeStruct((t, D_MODEL), F32),
        compiler_params=_params("parallel"),
        name="merge",
    )(x, gains, y_ret, y_mla, y_diff, wg, wb, wo)


def _memkv_kernel(mem_ref, g_ref, w_ref, o_ref):
    m = _rms(mem_ref[...], g_ref[2:3, :]).astype(BF16)
    o_ref[...] = _dot(m, w_ref[...]).astype(BF16)


def _memkv(mem, gains, wkv):
    b, n, _ = mem.shape
    depth = wkv.shape[0]
    return pl.pallas_call(
        _memkv_kernel,
        grid=(depth, b),
        in_specs=[pl.BlockSpec((None, n, D_MODEL), lambda l, i: (i, 0, 0)),
                  pl.BlockSpec((None, 3, D_MODEL), lambda l, i: (l, 0, 0)),
                  pl.BlockSpec((None, D_MODEL, 2 * D_MODEL), lambda l, i: (l, 0, 0))],
        out_specs=pl.BlockSpec((None, None, n, 2 * D_MODEL), lambda l, i: (l, i, 0, 0)),
        out_shape=jax.ShapeDtypeStruct((depth, b, n, 2 * D_MODEL), BF16),
        compiler_params=_params("parallel", "parallel"),
        name="memkv",
    )(mem, gains, wkv)


def _cross_kernel(x_ref, g_ref, kv_ref, wq_ref, wo_ref, o_ref):
    def query(n, _):
        h = _rms(x_ref[_row_block(n), :], g_ref[0:1, :]).astype(BF16)
        return (_dot(h, wq_ref[...]) * (LOG2E * CROSS_HD ** -0.5)).astype(BF16)

    def attend(n, q):
        heads = []
        for hd in range(CROSS_HEADS):
            cols = slice(hd * CROSS_HD, (hd + 1) * CROSS_HD)
            vcols = slice(D_MODEL + hd * CROSS_HD, D_MODEL + (hd + 1) * CROSS_HD)
            sc = _dot_nt(q[:, cols], kv_ref[:, cols])
            p = jnp.exp2(sc - jnp.max(sc, axis=-1, keepdims=True))
            total = jnp.sum(p, axis=-1, keepdims=True)
            heads.append((_dot(p.astype(BF16), kv_ref[:, vcols]) / total).astype(BF16))
        return jnp.concatenate(heads, axis=-1)

    def project(n, o):
        y = _dot(o, wo_ref[...])
        o_ref[_row_block(n), :] = x_ref[_row_block(n), :] + _rms(y, g_ref[1:2, :])

    _wavefront(x_ref.shape[0] // SUB_TILE, (query, attend, project))


def _cross(x, gains, kv, wq, wo, layer):
    b, s, _ = x.shape
    tm = DENSE_TILE
    n = kv.shape[2]
    return pl.pallas_call(
        _cross_kernel,
        grid=(b, s // tm),
        in_specs=[pl.BlockSpec((None, tm, D_MODEL), lambda i, j: (i, j, 0)),
                  _resident((3, D_MODEL), layer),
                  pl.BlockSpec((None, None, n, 2 * D_MODEL), lambda i, j: (layer, i, 0, 0)),
                  _resident((D_MODEL, D_MODEL), layer),
                  _resident((D_MODEL, D_MODEL), layer)],
        out_specs=pl.BlockSpec((None, tm, D_MODEL), lambda i, j: (i, j, 0)),
        out_shape=jax.ShapeDtypeStruct((b, s, D_MODEL), F32),
        compiler_params=_params("parallel", "parallel"),
        name="cross_attention",
    )(x, gains, kv, wq, wo)


K_ROPE_AT = 1920
REPACK_ROWS = 128


def _repack_kernel(w_ref, wa_ref, wg_ref):
    x = w_ref[...]
    rows = x.shape[0]
    wa_ref[:, 0:K_ROPE_AT] = x[:, 0:K_ROPE_AT].astype(BF16)
    lane = lax.broadcasted_iota(jnp.int32, (rows, LANES), 1)
    slot = pltpu.roll(x[:, K_ROPE_AT:K_ROPE_AT + LANES], MLA_NOPE, 1)
    in_slot = (lane >= MLA_NOPE) & (lane < MLA_NOPE + MLA_ROPE)
    wa_ref[:, K_ROPE_AT:_C_DQ] = jnp.where(in_slot, slot, 0.0).astype(BF16)
    wa_ref[:, _C_DQ:] = x[:, K_ROPE_AT + MLA_ROPE:GATE_OFFSET].astype(BF16)
    wg_ref[...] = x[:, GATE_OFFSET:].astype(BF16)


def _repack_w_in(w_in):
    depth, d, width = w_in.shape
    r = REPACK_ROWS
    return pl.pallas_call(
        _repack_kernel,
        grid=(depth, d // r),
        in_specs=[pl.BlockSpec((None, r, width), lambda l, i: (l, i, 0))],
        out_specs=[pl.BlockSpec((None, r, PROJ_WIDTH), lambda l, i: (l, i, 0)),
                   pl.BlockSpec((None, r, N_BRANCH * D_MODEL), lambda l, i: (l, i, 0))],
        out_shape=[jax.ShapeDtypeStruct((depth, d, PROJ_WIDTH), BF16),
                   jax.ShapeDtypeStruct((depth, d, N_BRANCH * D_MODEL), BF16)],
        compiler_params=_params("parallel", "parallel"),
        name="repack_w_in",
    )(w_in)


def _pack_inproj(w_in, wq_b, wkv_b):
    depth = w_in.shape[0]
    wa, wg = _repack_w_in(w_in)
    wq = wq_b.astype(BF16).reshape(depth, MLA_Q_RANK, MLA_HEADS, MLA_NOPE + MLA_ROPE)
    wq = jnp.pad(wq, ((0, 0), (0, 0), (0, 0), (0, LANES - MLA_NOPE - MLA_ROPE)))
    wq = wq.reshape(depth, MLA_Q_RANK, MLA_HEADS * LANES)
    wkv = wkv_b.astype(BF16).reshape(depth, MLA_KV_RANK, MLA_HEADS, MLA_NOPE + MLA_DV)
    wk = jnp.pad(wkv[..., :MLA_NOPE], ((0, 0), (0, 0), (0, 0), (0, LANES - MLA_NOPE)))
    wk = wk.reshape(depth, MLA_KV_RANK, MLA_HEADS * LANES)
    wv = wkv[..., MLA_NOPE:].reshape(depth, MLA_KV_RANK, MLA_HEADS * MLA_DV)
    return wa, wg, wq, wk, wv


def kernel(x, mem, positions, ffn1_norms, ffn1_w13, ffn1_w2, mix_norms, w_in, mla_q_norm,
           mla_kv_norm, mla_wq_b, mla_wkv_b, diff_lambda, w_branch, w_out, cross_norms,
           cross_wq, cross_wkv, cross_wo, ffn2_norms, ffn2_w13, ffn2_w2):
    b, s, d = x.shape
    t = b * s
    depth = w_in.shape[0]
    cos_tab, sin_tab = _rope_tables(positions)

    f1_w13, f1_w2 = ffn1_w13.astype(BF16), ffn1_w2.astype(BF16)
    f2_w13, f2_w2 = ffn2_w13.astype(BF16), ffn2_w2.astype(BF16)
    wa, wg, wqb, wkbk, wkbv = _pack_inproj(w_in, mla_wq_b, mla_wkv_b)
    wb, wo = w_branch.astype(BF16), w_out.astype(BF16)
    c_wq, c_wo = cross_wq.astype(BF16), cross_wo.astype(BF16)
    q_norm = mla_q_norm.reshape(depth, 1, MLA_Q_RANK)
    kv_norm = mla_kv_norm.reshape(depth, 1, MLA_KV_RANK)

    mem_kv = _memkv(mem, cross_norms, cross_wkv.astype(BF16))

    def seq(a):
        return a.reshape(b, s, a.shape[-1])

    xf = x.reshape(t, d)
    for l in range(depth):
        lambda_init = 0.8 - 0.6 * math.exp(-0.3 * l)
        xf = _ffn(xf, ffn1_norms, f1_w13, f1_w2, l)

        (rq, rk, rv, rg, mq, mk, mv, dq, dk, dv) = _inproj(
            xf, mix_norms, wa, q_norm, kv_norm, wqb, wkbk, wkbv, cos_tab, sin_tab, l)
        y_ret = _retention(seq(rq), seq(rk), seq(rv), seq(rg))
        y_mla = _mla_attention(seq(mq), seq(mk), seq(mv))
        y_diff = _diff_attention(seq(dq), seq(dk), seq(dv), diff_lambda[l], lambda_init)
        xf = _merge(xf, mix_norms, y_ret.reshape(t, -1), y_mla.reshape(t, -1),
                    y_diff.reshape(t, -1), wg, wb, wo, l)

        xf = _cross(xf.reshape(b, s, d), cross_norms, mem_kv, c_wq, c_wo, l).reshape(t, d)

        xf = _ffn(xf, ffn2_norms, f2_w13, f2_w2, l)
    return xf.reshape(b, s, d)
```

```python
import functools
import math

import numpy as np
import jax
import jax.numpy as jnp
from jax import lax
from jax.experimental import pallas as pl
from jax.experimental.pallas import tpu as pltpu

F32 = jnp.float32
BF16 = jnp.bfloat16

D_MODEL = 1024
DEPTH = 4
CHUNK = 64
EPS = 1e-6
NEG_INF = -1e30
ROPE_THETA = 500000.0
RET_THETA = 10000.0

RET_HEADS = 4
RET_DK = 64
RET_DV = 128
MLA_HEADS = 8
MLA_Q_RANK = 256
MLA_KV_RANK = 128
MLA_NOPE = 64
MLA_ROPE = 32
MLA_DV = 64
DIFF_HEADS = 4
DIFF_HD = 64
DIFF_ROT = 16
N_BRANCH = 3
BRANCH_WIDTH = 512
CROSS_HEADS = 4
CROSS_HD = 256
D_FF = 2816
GATE_OFFSET = 3488

LANES = 128
VMEM_LIMIT = 56 * 1024 * 1024

TOKEN_TILE = 512
DENSE_TILE = 1024
SUB_TILE = 256
ATTN_TILE = 256
RET_TILE = 256

LOG2E = math.log2(math.e)
RET_LOG_GAMMA =tuple(math.log(1.0 - 2.0 ** (-5.0 - h)) for h in range(RET_HEADS))


def _params(*sem):
    return pltpu.CompilerParams(dimension_semantics=sem, vmem_limit_bytes=VMEM_LIMIT)


def _rms(x, gain=None):
    y = x * lax.rsqrt(jnp.mean(x * x, axis=-1, keepdims=True) + EPS)
    return y if gain is None else y * gain


def _dot(a, b):
    return jnp.dot(a, b, preferred_element_type=F32)


def _dot_nt(a, b):
    return lax.dot_general(a, b, (((1,), (1,)), ((), ())), preferred_element_type=F32)


def _rope(x, cos, sin_signed, half):
    lane = lax.broadcasted_iota(jnp.int32, x.shape, 1)
    first = (lane & (2 * half - 1)) < half
    partner = jnp.where(first, pltpu.roll(x, LANES - half, 1), pltpu.roll(x, half, 1))
    return x * cos + partner * sin_signed


def _rope_patterns():
    inv = np.zeros((1, LANES), np.float32)
    inv[0, 0:32] = 1.0 / (RET_THETA ** (np.arange(0, RET_DK, 2, dtype=np.float32) / RET_DK))
    inv[0, 32:48] = 1.0 / (ROPE_THETA ** (np.arange(0, MLA_ROPE, 2, dtype=np.float32) / MLA_ROPE))
    inv[0, 48:56] = 1.0 / (ROPE_THETA ** (np.arange(0, DIFF_ROT, 2, dtype=np.float32) / DIFF_ROT))
    shift = np.full((3, LANES), -1, np.int32)
    sign = np.zeros((3, LANES), np.float32)
    for i in range(LANES):
        j = i % RET_DK
        shift[0, i] = (i - j % 32) % LANES
        sign[0, i] = -1.0 if j < 32 else 1.0
        if 64 <= i < 96:
            j = i - 64
            shift[1, i] = (i - (32 + j % 16)) % LANES
            sign[1, i] = -1.0 if j < 16 else 1.0
        j = i % DIFF_HD
        if j < DIFF_ROT:
            shift[2, i] = (i - (48 + j % 8)) % LANES
            sign[2, i] = -1.0 if j < 8 else 1.0
    return inv, shift, sign


_ROPE_INV, _ROPE_SHIFT, _ROPE_SIGN = _rope_patterns()


def _rope_table_kernel(pos_ref, inv_ref, shift_ref, sign_ref, cos_ref, sin_ref):
    ang = pos_ref[...].astype(F32) * inv_ref[...]
    packed = {"cos": jnp.cos(ang), "sin": jnp.sin(ang)}
    rolled = {}

    def moved(name, k):
        if k == 0:
            return packed[name]
        if (name, k) not in rolled:
            rolled[name, k] = pltpu.roll(packed[name], k, 1)
        return rolled[name, k]

    for t in range(3):
        lane_shift = shift_ref[t:t + 1, :]
        cos_t = jnp.ones_like(ang)
        sin_t = jnp.zeros_like(ang)
        for k in sorted(set(int(v) for v in _ROPE_SHIFT[t] if v >= 0)):
            cos_t = jnp.where(lane_shift == k, moved("cos", k), cos_t)
            sin_t = jnp.where(lane_shift == k, moved("sin", k), sin_t)
        cos_ref[t] = cos_t
        sin_ref[t] = sin_t * sign_ref[t:t + 1, :]


def _rope_tables(positions):
    t = positions.size
    tm = TOKEN_TILE
    tab = jax.ShapeDtypeStruct((3, t, LANES), F32)
    return pl.pallas_call(
        _rope_table_kernel,
        grid=(t // tm,),
        in_specs=[pl.BlockSpec((tm, 1), lambda i: (i, 0)),
                  pl.BlockSpec((1, LANES), lambda i: (0, 0)),
                  pl.BlockSpec((3, LANES), lambda i: (0, 0)),
                  pl.BlockSpec((3, LANES), lambda i: (0, 0))],
        out_specs=[pl.BlockSpec((3, tm, LANES), lambda i: (0, i, 0)),
                   pl.BlockSpec((3, tm, LANES), lambda i: (0, i, 0))],
        out_shape=[tab, tab],
        compiler_params=_params("parallel"),
        name="rope_tables",
    )(positions.reshape(t, 1), jnp.asarray(_ROPE_INV), jnp.asarray(_ROPE_SHIFT),
      jnp.asarray(_ROPE_SIGN))


def _wavefront(n_items, stages):
    state = [None] * n_items
    for wave in range(n_items + len(stages) - 1):
        for s in reversed(range(len(stages))):
            n = wave - s
            if 0 <= n < n_items:
                state[n] = stages[s](n, state[n])


def _row_block(n):
    return pl.ds(n * SUB_TILE, SUB_TILE)


def _ffn_kernel(x_ref, g_ref, w13_ref, w2_ref, o_ref):
    def pre_norm(n, _):
        return _rms(x_ref[_row_block(n), :], g_ref[0:1, :]).astype(BF16)

    def hidden(n, h):
        gate = _dot(h, w13_ref[:, :D_FF])
        up = _dot(h, w13_ref[:, D_FF:])
        return (gate * jax.nn.sigmoid(gate) * up).astype(BF16)

    def project(n, act):
        y = _dot(act, w2_ref[...])
        o_ref[_row_block(n), :] = x_ref[_row_block(n), :] + 0.5 * _rms(y, g_ref[1:2, :])

    _wavefront(x_ref.shape[0] // SUB_TILE, (pre_norm, hidden, project))


def _resident(shape, layer):
    zeros = (0,) * len(shape)
    return pl.BlockSpec((None,) + tuple(shape), lambda *_: (layer,) + zeros,
                        pipeline_mode=pl.Buffered(1))


def _ffn(x, gains, w13, w2, layer):
    t = x.shape[0]
    tm = DENSE_TILE
    return pl.pallas_call(
        _ffn_kernel,
        grid=(t // tm,),
        in_specs=[pl.BlockSpec((tm, D_MODEL), lambda i: (i, 0)),
                  _resident((2, D_MODEL), layer),
                  _resident((D_MODEL, 2 * D_FF), layer),
                  _resident((D_FF, D_MODEL), layer)],
        out_specs=pl.BlockSpec((tm, D_MODEL), lambda i: (i, 0)),
        out_shape=jax.ShapeDtypeStruct((t, D_MODEL), F32),
        compiler_params=_params("parallel"),
        name="ffn",
    )(x, gains, w13, w2)


_C_RQ, _C_RK, _C_RV, _C_RG = 0, 256, 512, 1024
_C_MQ, _C_MKV = 1536, 1792
_C_DQ, _C_DK, _C_DV = 2048, 2560, 3072
PROJ_WIDTH = 3584


def _inproj_kernel(x_ref, g_ref, wa_ref, qn_ref, kvn_ref, wqb_ref, wkbk_ref, wkbv_ref,
                   cos_ref, sin_ref,
                   rq_ref, rk_ref, rv_ref, rg_ref, mq_ref, mk_ref, mv_ref,
                   dq_ref, dk_ref, dv_ref):
    h = _rms(x_ref[...], g_ref[0:1, :]).astype(BF16)

    def proj(start, width):
        return _dot_nt(h, wa_ref[start:start + width, :])

    def rope_store(dst, val, table, half, scale):
        cos, sin = cos_ref[table], sin_ref[table]
        for c in range(val.shape[1] // LANES):
            sl = slice(c * LANES, (c + 1) * LANES)
            r = _rope(val[:, sl], cos, sin, half)
            if scale != 1.0:
                r = r * scale
            dst[:, sl] = r.astype(BF16)

    def cast_store(dst):
        def store(val):
            dst[...] = val.astype(BF16)
        return store

    def silu_store(val):
        rg_ref[...] = (val * jax.nn.sigmoid(val)).astype(BF16)

    low_rank = {}

    def keep_cq(val):
        low_rank["cq"] = _rms(val, qn_ref[...]).astype(BF16)

    def keep_kv(val):
        low_rank["ckv"] = _rms(val[:, :MLA_KV_RANK], kvn_ref[...]).astype(BF16)
        low_rank["k_rope"] = _rope(val[:, MLA_KV_RANK:], cos_ref[1], sin_ref[1], MLA_ROPE // 2)

    def key_store(k_nope):
        for hd in range(MLA_HEADS):
            sl = slice(hd * LANES, (hd + 1) * LANES)
            mk_ref[:, sl] = (k_nope[:, sl] + low_rank["k_rope"]).astype(BF16)

    def roped(dst, table, half, scale):
        return lambda val: rope_store(dst, val, table, half, scale)

    items = (
        (lambda: proj(_C_RQ, 256), roped(rq_ref, 0, RET_DK // 2, RET_DK ** -0.5)),
        (lambda: proj(_C_MQ, MLA_Q_RANK), keep_cq),
        (lambda: proj(_C_RK, 256), roped(rk_ref, 0, RET_DK // 2, 1.0)),
        (lambda: proj(_C_MKV, MLA_KV_RANK + LANES), keep_kv),
        (lambda: _dot(low_rank["cq"], wqb_ref[...]),
         roped(mq_ref, 1, MLA_ROPE // 2, LOG2E * (MLA_NOPE + MLA_ROPE) ** -0.5)),
        (lambda: proj(_C_RV, 512), cast_store(rv_ref)),
        (lambda: _dot(low_rank["ckv"], wkbk_ref[...]), key_store),
        (lambda: proj(_C_RG, 512), silu_store),
        (lambda: _dot(low_rank["ckv"], wkbv_ref[...]), cast_store(mv_ref)),
        (lambda: proj(_C_DQ, 512), roped(dq_ref, 2, DIFF_ROT // 2, LOG2E * DIFF_HD ** -0.5)),
        (lambda: proj(_C_DK, 512), roped(dk_ref, 2, DIFF_ROT // 2, 1.0)),
        (lambda: proj(_C_DV, 512), cast_store(dv_ref)),
    )
    _skewed(len(items), lambda n: items[n][0](), lambda n, val: items[n][1](val))


def _inproj(x, gains, wa, q_norm, kv_norm, wqb, wkbk, wkbv, cos_tab, sin_tab, layer):
    t = x.shape[0]
    tm = TOKEN_TILE
    widths = (256, 256, 512, 512, 1024, 1024, 512, 512, 512, 512)

    return pl.pallas_call(
        _inproj_kernel,
        grid=(t // tm,),
        in_specs=[pl.BlockSpec((tm, D_MODEL), lambda i: (i, 0)),
                  _resident((2, D_MODEL), layer),
                  _resident((PROJ_WIDTH, D_MODEL), layer),
                  _resident((1, MLA_Q_RANK), layer),
                  _resident((1, MLA_KV_RANK), layer),
                  _resident((MLA_Q_RANK, MLA_HEADS * LANES), layer),
                  _resident((MLA_KV_RANK, MLA_HEADS * LANES), layer),
                  _resident((MLA_KV_RANK, MLA_HEADS * MLA_DV), layer),
                  pl.BlockSpec((3, tm, LANES), lambda i: (0, i, 0)),
                  pl.BlockSpec((3, tm, LANES), lambda i: (0, i, 0))],
        out_specs=[pl.BlockSpec((tm, w), lambda i: (i, 0)) for w in widths],
        out_shape=[jax.ShapeDtypeStruct((t, w), BF16) for w in widths],
        compiler_params=_params("parallel"),
        name="inproj",
    )(x, gains, wa, q_norm, kv_norm, wqb, wkbk, wkbv, cos_tab, sin_tab)


def _retention_kernel(q_ref, k_ref, v_ref, g_ref, o_ref, state_sc, decay_sc, qk_dec_sc):
    r = RET_TILE
    state_sc[...] = jnp.zeros_like(state_sc)

    row = lax.broadcasted_iota(jnp.int32, (r, r), 0)
    col = lax.broadcasted_iota(jnp.int32, (r, r), 1)
    allowed = (col >> 6) <= (row >> 6)
    dist = jnp.abs(row - col).astype(F32)
    for h in range(RET_HEADS):
        decay_sc[h] = jnp.where(allowed, jnp.exp(RET_LOG_GAMMA[h] * dist), 0.0)
    lane = lax.broadcasted_iota(jnp.int32, (r, LANES), 1)
    first_head = lane < RET_DK
    n_local = lax.broadcasted_iota(jnp.int32, (r, LANES), 0).astype(F32)
    for p in range(RET_HEADS // 2):
        lg_lane = jnp.where(first_head, RET_LOG_GAMMA[2 * p], RET_LOG_GAMMA[2 * p + 1])
        qk_dec_sc[p, 0] = jnp.exp(lg_lane * (n_local + 1.0))
        qk_dec_sc[p, 1] = jnp.exp(lg_lane * (r - 1.0 - n_local))
    s_row = lax.broadcasted_iota(jnp.int32, (LANES, 2 * RET_DV), 0)
    s_col = lax.broadcasted_iota(jnp.int32, (LANES, 2 * RET_DV), 1)
    same_head = (s_row < RET_DK) == (s_col < RET_DV)

    def block(t, carry):
        rows = pl.ds(pl.multiple_of(t * r, r), r)
        for p in range(RET_HEADS // 2):
            lg0, lg1 = RET_LOG_GAMMA[2 * p], RET_LOG_GAMMA[2 * p + 1]
            q = q_ref[rows, p * LANES:(p + 1) * LANES]
            k = k_ref[rows, p * LANES:(p + 1) * LANES]
            v = v_ref[rows, 2 * p * RET_DV:2 * (p + 1) * RET_DV]
            state = state_sc[p]

            q_dec = (q.astype(F32) * qk_dec_sc[p, 0]).astype(BF16)
            o_cross = _dot(q_dec, state.astype(BF16))

            for hh in range(2):
                qm = jnp.where(first_head != bool(hh), q, jnp.zeros_like(q))
                scores = (_dot_nt(qm, k) * decay_sc[2 * p + hh]).astype(BF16)
                sl = slice(hh * RET_DV, (hh + 1) * RET_DV)
                o = _dot(scores, v[:, sl]) + o_cross[:, sl]
                gsl = slice((2 * p + hh) * RET_DV, (2 * p + hh + 1) * RET_DV)
                o_ref[rows, gsl] = (_rms(o) * g_ref[rows, gsl].astype(F32)).astype(BF16)

            k_dec = k.astype(F32) * qk_dec_sc[p, 1]
            kv = _dot(k_dec.T.astype(BF16), v)
            block_decay = jnp.where(s_col < RET_DV, math.exp(lg0 * r), math.exp(lg1 * r))
            state_sc[p] = state * block_decay + jnp.where(same_head, kv, 0.0)
        return carry

    lax.fori_loop(0, q_ref.shape[0] // r, block, 0)


def _retention(q, k, v, g):
    b, s, _ = q.shape
    r = RET_TILE

    def spec(w):
        return pl.BlockSpec((None, s, w), lambda i: (i, 0, 0))

    return pl.pallas_call(
        _retention_kernel,
        grid=(b,),
        in_specs=[spec(256), spec(256), spec(512), spec(512)],
        out_specs=spec(512),
        out_shape=jax.ShapeDtypeStruct((b, s, BRANCH_WIDTH), BF16),
        scratch_shapes=[pltpu.VMEM((RET_HEADS // 2, LANES, 2 * RET_DV), F32),
                        pltpu.VMEM((RET_HEADS, r, r), F32),
                        pltpu.VMEM((RET_HEADS // 2, 2, r, LANES), F32)],
        compiler_params=_params("parallel"),
        name="retention",
    )(q, k, v, g)


ONES_ROWS = 16


def _scores_t(q, k_ref, lo, bias_t):
    tq = q.shape[0]
    s_diag = _dot_nt(k_ref[lo:lo + tq, :], q) + bias_t
    m = jnp.max(s_diag, axis=0, keepdims=True)
    s_past = None
    if lo > 0:
        s_past = _dot_nt(k_ref[0:lo, :], q)
        m = jnp.maximum(m, jnp.max(s_past, axis=0, keepdims=True))
    return s_diag, s_past, m


def _weighted_values_t(scores, vt_ref, lo):
    s_diag, s_past, m = scores
    tq = s_diag.shape[1]
    out = _dot(vt_ref[:, lo:lo + tq], jnp.exp2(s_diag - m).astype(BF16))
    for t in range(lo // tq):
        keys = slice(t * tq, (t + 1) * tq)
        out = out + _dot(vt_ref[:, keys], jnp.exp2(s_past[keys, :] - m).astype(BF16))
    return out


def _skewed(n_items, first_stage, second_stage):
    pending = first_stage(0)
    for n in range(n_items):
        ahead = first_stage(n + 1) if n + 1 < n_items else None
        second_stage(n, pending)
        pending = ahead


def _chunk_bias_t(n):
    key = lax.broadcasted_iota(jnp.int32, (n, n), 0)
    qry = lax.broadcasted_iota(jnp.int32, (n, n), 1)
    return jnp.where((key >> 6) <= (qry >> 6), 0.0, NEG_INF).astype(F32)


def _mla_attn_kernel(q_ref, k_ref, v_ref, o_ref, vt_sc):
    tq = ATTN_TILE
    s = q_ref.shape[0]
    vt = v_ref[...].astype(F32).T
    for hh in range(2):
        vt_sc[hh, 0:MLA_DV, :] = vt[hh * MLA_DV:(hh + 1) * MLA_DV, :].astype(BF16)
        vt_sc[hh, MLA_DV:, :] = jnp.ones((ONES_ROWS, s), BF16)
    bias_t = _chunk_bias_t(tq)

    def scores(i):
        lo = i * tq
        return [_scores_t(q_ref[lo:lo + tq, hh * LANES:(hh + 1) * LANES],
                          k_ref.at[:, hh * LANES:(hh + 1) * LANES], lo, bias_t) for hh in range(2)]

    def outputs(i, sc):
        lo = i * tq
        accs = [_weighted_values_t(sc[hh], vt_sc.at[hh], lo) for hh in range(2)]
        halves = [acc[0:MLA_DV, :] / acc[MLA_DV:MLA_DV + 1, :] for acc in accs]
        o_ref[lo:lo + tq, :] = jnp.concatenate(halves, axis=0).T.astype(BF16)

    _skewed(s // tq, scores, outputs)


def _mla_attention(q, k, v):
    b, s, _ = q.shape
    pairs = MLA_HEADS // 2
    return pl.pallas_call(
        _mla_attn_kernel,
        grid=(b, pairs),
        in_specs=[pl.BlockSpec((None, s, 2 * LANES), lambda i, j: (i, 0, j)),
                  pl.BlockSpec((None, s, 2 * LANES), lambda i, j: (i, 0, j)),
                  pl.BlockSpec((None, s, LANES), lambda i, j: (i, 0, j))],
        out_specs=pl.BlockSpec((None, s, LANES), lambda i, j: (i, 0, j)),
        out_shape=jax.ShapeDtypeStruct((b, s, BRANCH_WIDTH), BF16),
        scratch_shapes=[pltpu.VMEM((2, MLA_DV + ONES_ROWS, s), BF16)],
        compiler_params=_params("parallel", "parallel"),
        name="mla_attention",
    )(q, k, v)


def _diff_attn_kernel(q_ref, k_ref, v_ref, lam_ref, linit_ref, o_ref, vt_sc):
    tq = ATTN_TILE
    s = q_ref.shape[0]
    dv = 2 * DIFF_HD
    vt_sc[0:dv, :] = v_ref[...].astype(F32).T.astype(BF16)
    vt_sc[dv:, :] = jnp.ones((ONES_ROWS, s), BF16)
    bias_t = _chunk_bias_t(tq)
    lane = lax.broadcasted_iota(jnp.int32, (tq, LANES), 1)
    lp = lam_ref[...]
    linit = linit_ref[...]
    lam = (jnp.exp(jnp.sum(lp[0:1] * lp[1:2], axis=-1, keepdims=True))
           - jnp.exp(jnp.sum(lp[2:3] * lp[3:4], axis=-1, keepdims=True)) + linit)
    def scores(i):
        lo = i * tq
        q = q_ref[lo:lo + tq, :]
        zero = jnp.zeros_like(q)
        return [_scores_t(jnp.where((lane < DIFF_HD) != bool(j), q, zero), k_ref, lo, bias_t)
                for j in range(2)]

    def outputs(i, sc):
        lo = i * tq
        accs = [_weighted_values_t(sc[j], vt_sc, lo) for j in range(2)]
        maps = [acc[0:dv, :] / acc[dv:dv + 1, :] for acc in accs]
        o_t = maps[0] - lam * maps[1]
        o_t = o_t * lax.rsqrt(jnp.mean(o_t * o_t, axis=0, keepdims=True) + EPS)
        o_ref[lo:lo + tq, :] = (o_t * (1.0 - linit)).T.astype(BF16)

    _skewed(s // tq, scores, outputs)


def _diff_attention(q, k, v, lam_params, lambda_init):
    b, s, _ = q.shape

    def spec():
        return pl.BlockSpec((None, s, LANES), lambda i, j: (i, 0, j))

    return pl.pallas_call(
        _diff_attn_kernel,
        grid=(b, DIFF_HEADS),
        in_specs=[spec(), spec(), spec(),
                  pl.BlockSpec((4, DIFF_HD), lambda i, j: (0, 0)),
                  pl.BlockSpec((1, 1), lambda i, j: (0, 0))],
        out_specs=spec(),
        out_shape=jax.ShapeDtypeStruct((b, s, BRANCH_WIDTH), BF16),
        scratch_shapes=[pltpu.VMEM((2 * DIFF_HD + ONES_ROWS, s), BF16)],
        compiler_params=_params("parallel", "parallel"),
        name="diff_attention",
    )(q, k, v, lam_params, jnp.full((1, 1), lambda_init, F32))


def _merge_kernel(x_ref, g_ref, yr_ref, ym_ref, yd_ref, wg_ref, wb_ref, wo_ref, o_ref):
    def pre_norm(n, _):
        return _rms(x_ref[_row_block(n), :], g_ref[0:1, :]).astype(BF16)

    def gated_sum(n, h):
        mixed = None
        for br, y_ref in enumerate((yr_ref, ym_ref, yd_ref)):
            gate = jax.nn.sigmoid(_dot_nt(h, wg_ref[br * D_MODEL:(br + 1) * D_MODEL, :]))
            term = gate * _dot(y_ref[_row_block(n), :], wb_ref[br])
            mixed = term if mixed is None else mixed + term
        return mixed.astype(BF16)

    def project(n, mixed):
        y = _dot(mixed, wo_ref[...])
        o_ref[_row_block(n), :] = x_ref[_row_block(n), :] + _rms(y, g_ref[1:2, :])

    _wavefront(x_ref.shape[0] // SUB_TILE, (pre_norm, gated_sum, project))


def _merge(x, gains, y_ret, y_mla, y_diff, wg, wb, wo, layer):
    t = x.shape[0]
    tm = DENSE_TILE

    def rows(w):
        return pl.BlockSpec((tm, w), lambda i: (i, 0))

    return pl.pallas_call(
        _merge_kernel,
        grid=(t // tm,),
        in_specs=[rows(D_MODEL),
                  _resident((2, D_MODEL), layer),
                  rows(BRANCH_WIDTH), rows(BRANCH_WIDTH), rows(BRANCH_WIDTH),
                  _resident((N_BRANCH * D_MODEL, D_MODEL), layer),
                  _resident((N_BRANCH, BRANCH_WIDTH, D_MODEL), layer),
                  _resident((D_MODEL, D_MODEL), layer)],
        out_specs=rows(D_MODEL),
        out_shape=jax.ShapeDtypeStruct((t, D_MODEL), F32),
        compiler_params=_params("parallel"),
        name="merge",
    )(x, gains, y_ret, y_mla, y_diff, wg, wb, wo)


def _memkv_kernel(mem_ref, g_ref, w_ref, o_ref):
    m = _rms(mem_ref[...], g_ref[2:3, :]).astype(BF16)
    o_ref[...] = _dot(m, w_ref[...]).astype(BF16)


def _memkv(mem, gains, wkv):
    b, n, _ = mem.shape
    depth = wkv.shape[0]
    return pl.pallas_call(
        _memkv_kernel,
        grid=(depth, b),
        in_specs=[pl.BlockSpec((None, n, D_MODEL), lambda l, i: (i, 0, 0)),
                  pl.BlockSpec((None, 3, D_MODEL), lambda l, i: (l, 0, 0)),
                  pl.BlockSpec((None, D_MODEL, 2 * D_MODEL), lambda l, i: (l, 0, 0))],
        out_specs=pl.BlockSpec((None, None, n, 2 * D_MODEL), lambda l, i: (l, i, 0, 0)),
        out_shape=jax.ShapeDtypeStruct((depth, b, n, 2 * D_MODEL), BF16),
        compiler_params=_params("parallel", "parallel"),
        name="memkv",
    )(mem, gains, wkv)


def _cross_kernel(x_ref, g_ref, kv_ref, wq_ref, wo_ref, o_ref):
    def query(n, _):
        h = _rms(x_ref[_row_block(n), :], g_ref[0:1, :]).astype(BF16)
        return (_dot(h, wq_ref[...]) * (LOG2E * CROSS_HD ** -0.5)).astype(BF16)

    def attend(n, q):
        heads = []
        for hd in range(CROSS_HEADS):
            cols = slice(hd * CROSS_HD, (hd + 1) * CROSS_HD)
            vcols = slice(D_MODEL + hd * CROSS_HD, D_MODEL + (hd + 1) * CROSS_HD)
            sc = _dot_nt(q[:, cols], kv_ref[:, cols])
            p = jnp.exp2(sc - jnp.max(sc, axis=-1, keepdims=True))
            total = jnp.sum(p, axis=-1, keepdims=True)
            heads.append((_dot(p.astype(BF16), kv_ref[:, vcols]) / total).astype(BF16))
        return jnp.concatenate(heads, axis=-1)

    def project(n, o):
        y = _dot(o, wo_ref[...])
        o_ref[_row_block(n), :] = x_ref[_row_block(n), :] + _rms(y, g_ref[1:2, :])

    _wavefront(x_ref.shape[0] // SUB_TILE, (query, attend, project))


def _cross(x, gains, kv, wq, wo, layer):
    b, s, _ = x.shape
    tm = DENSE_TILE
    n = kv.shape[2]
    return pl.pallas_call(
        _cross_kernel,
        grid=(b, s // tm),
        in_specs=[pl.BlockSpec((None, tm, D_MODEL), lambda i, j: (i, j, 0)),
                  _resident((3, D_MODEL), layer),
                  pl.BlockSpec((None, None, n, 2 * D_MODEL), lambda i, j: (layer, i, 0, 0)),
                  _resident((D_MODEL, D_MODEL), layer),
                  _resident((D_MODEL, D_MODEL), layer)],
        out_specs=pl.BlockSpec((None, tm, D_MODEL), lambda i, j: (i, j, 0)),
        out_shape=jax.ShapeDtypeStruct((b, s, D_MODEL), F32),
        compiler_params=_params("parallel", "parallel"),
        name="cross_attention",
    )(x, gains, kv, wq, wo)


K_ROPE_AT = 1920


def _pack_inproj(w_in, wq_b, wkv_b):
    depth, d, _ = w_in.shape
    w_t = jnp.swapaxes(w_in, 1, 2)
    wa = jnp.concatenate(
        [w_t[:, :K_ROPE_AT].astype(BF16),
         jnp.zeros((depth, MLA_NOPE, d), BF16),
         w_t[:, K_ROPE_AT:K_ROPE_AT + MLA_ROPE].astype(BF16),
         jnp.zeros((depth, LANES - MLA_NOPE - MLA_ROPE, d), BF16),
         w_t[:, K_ROPE_AT + MLA_ROPE:GATE_OFFSET].astype(BF16)], axis=1)
    wg = w_t[:, GATE_OFFSET:].astype(BF16)
    wq = wq_b.astype(BF16).reshape(depth, MLA_Q_RANK, MLA_HEADS, MLA_NOPE + MLA_ROPE)
    wq = jnp.pad(wq, ((0, 0), (0, 0), (0, 0), (0, LANES - MLA_NOPE - MLA_ROPE)))
    wq = wq.reshape(depth, MLA_Q_RANK, MLA_HEADS * LANES)
    wkv = wkv_b.astype(BF16).reshape(depth, MLA_KV_RANK, MLA_HEADS, MLA_NOPE + MLA_DV)
    wk = jnp.pad(wkv[..., :MLA_NOPE], ((0, 0), (0, 0), (0, 0), (0, LANES - MLA_NOPE)))
    wk = wk.reshape(depth, MLA_KV_RANK, MLA_HEADS * LANES)
    wv = wkv[..., MLA_NOPE:].reshape(depth, MLA_KV_RANK, MLA_HEADS * MLA_DV)
    return wa, wg, wq, wk, wv


def kernel(x, mem, positions, ffn1_norms, ffn1_w13, ffn1_w2, mix_norms, w_in, mla_q_norm,
           mla_kv_norm, mla_wq_b, mla_wkv_b, diff_lambda, w_branch, w_out, cross_norms,
           cross_wq, cross_wkv, cross_wo, ffn2_norms, ffn2_w13, ffn2_w2):
    b, s, d = x.shape
    t = b * s
    depth = w_in.shape[0]
    cos_tab, sin_tab = _rope_tables(positions)

    f1_w13, f1_w2 = ffn1_w13.astype(BF16), ffn1_w2.astype(BF16)
    f2_w13, f2_w2 = ffn2_w13.astype(BF16), ffn2_w2.astype(BF16)
    wa, wg, wqb, wkbk, wkbv = _pack_inproj(w_in, mla_wq_b, mla_wkv_b)
    wb, wo = w_branch.astype(BF16), w_out.astype(BF16)
    c_wq, c_wo = cross_wq.astype(BF16), cross_wo.astype(BF16)
    q_norm = mla_q_norm.reshape(depth, 1, MLA_Q_RANK)
    kv_norm = mla_kv_norm.reshape(depth, 1, MLA_KV_RANK)

    mem_kv = _memkv(mem, cross_norms, cross_wkv.astype(BF16))

    def seq(a):
        return a.reshape(b, s, a.shape[-1])

    xf = x.reshape(t, d)
    for l in range(depth):
        lambda_init = 0.8 - 0.6 * math.exp(-0.3 * l)
        xf = _ffn(xf, ffn1_norms, f1_w13, f1_w2, l)

        (rq, rk, rv, rg, mq, mk, mv, dq, dk, dv) = _inproj(
            xf, mix_norms, wa, q_norm, kv_norm, wqb, wkbk, wkbv, cos_tab, sin_tab, l)
        y_ret = _retention(seq(rq), seq(rk), seq(rv), seq(rg))
        y_mla = _mla_attention(seq(mq), seq(mk), seq(mv))
        y_diff = _diff_attention(seq(dq), seq(dk), seq(dv), diff_lambda[l], lambda_init)
        xf = _merge(xf, mix_norms, y_ret.reshape(t, -1), y_mla.reshape(t, -1),
                    y_diff.reshape(t, -1), wg, wb, wo, l)

        xf = _cross(xf.reshape(b, s, d), cross_norms, mem_kv, c_wq, c_wo, l).reshape(t, d)

        xf = _ffn(xf, ffn2_norms, f2_w13, f2_w2, l)
    return xf.reshape(b, s, d)
```

```python
import functools
import math

import numpy as np
import jax
import jax.numpy as jnp
from jax import lax
from jax.experimental import pallas as pl
from jax.experimental.pallas import tpu as pltpu

F32 = jnp.float32
BF16 = jnp.bfloat16

D_MODEL = 1024
DEPTH = 4
CHUNK = 64
EPS = 1e-6
NEG_INF = -1e30
ROPE_THETA = 500000.0
RET_THETA = 10000.0

RET_HEADS = 4
RET_DK = 64
RET_DV = 128
MLA_HEADS = 8
MLA_Q_RANK = 256
MLA_KV_RANK = 128
MLA_NOPE = 64
MLA_ROPE = 32
MLA_DV = 64
DIFF_HEADS = 4
DIFF_HD = 64
DIFF_ROT = 16
N_BRANCH = 3
BRANCH_WIDTH = 512
CROSS_HEADS = 4
CROSS_HD = 256
D_FF = 2816
GATE_OFFSET = 3488

LANES = 128
VMEM_LIMIT = 56 * 1024 * 1024

TOKEN_TILE = 512
DENSE_TILE = 1024
SUB_TILE = 256
ATTN_TILE = 256
RET_TILE = 256

LOG2E = math.log2(math.e)
RET_LOG_GAMMA =tuple(math.log(1.0 - 2.0 ** (-5.0 - h)) for h in range(RET_HEADS))


def _params(*sem):
    return pltpu.CompilerParams(dimension_semantics=sem, vmem_limit_bytes=VMEM_LIMIT)


def _rms(x, gain=None):
    y = x * lax.rsqrt(jnp.mean(x * x, axis=-1, keepdims=True) + EPS)
    return y if gain is None else y * gain


def _dot(a, b):
    return jnp.dot(a, b, preferred_element_type=F32)


def _dot_nt(a, b):
    return lax.dot_general(a, b, (((1,), (1,)), ((), ())), preferred_element_type=F32)


def _rope(x, cos, sin_signed, half):
    lane = lax.broadcasted_iota(jnp.int32, x.shape, 1)
    first = (lane & (2 * half - 1)) < half
    partner = jnp.where(first, pltpu.roll(x, LANES - half, 1), pltpu.roll(x, half, 1))
    return x * cos + partner * sin_signed


def _rope_patterns():
    inv = np.zeros((1, LANES), np.float32)
    inv[0, 0:32] = 1.0 / (RET_THETA ** (np.arange(0, RET_DK, 2, dtype=np.float32) / RET_DK))
    inv[0, 32:48] = 1.0 / (ROPE_THETA ** (np.arange(0, MLA_ROPE, 2, dtype=np.float32) / MLA_ROPE))
    inv[0, 48:56] = 1.0 / (ROPE_THETA ** (np.arange(0, DIFF_ROT, 2, dtype=np.float32) / DIFF_ROT))
    shift = np.full((3, LANES), -1, np.int32)
    sign = np.zeros((3, LANES), np.float32)
    for i in range(LANES):
        j = i % RET_DK
        shift[0, i] = (i - j % 32) % LANES
        sign[0, i] = -1.0 if j < 32 else 1.0
        if 64 <= i < 96:
            j = i - 64
            shift[1, i] = (i - (32 + j % 16)) % LANES
            sign[1, i] = -1.0 if j < 16 else 1.0
        j = i % DIFF_HD
        if j < DIFF_ROT:
            shift[2, i] = (i - (48 + j % 8)) % LANES
            sign[2, i] = -1.0 if j < 8 else 1.0
    return inv, shift, sign


_ROPE_INV, _ROPE_SHIFT, _ROPE_SIGN = _rope_patterns()


def _rope_table_kernel(pos_ref, inv_ref, shift_ref, sign_ref, cos_ref, sin_ref):
    ang = pos_ref[...].astype(F32) * inv_ref[...]
    packed = {"cos": jnp.cos(ang), "sin": jnp.sin(ang)}
    rolled = {}

    def moved(name, k):
        if k == 0:
            return packed[name]
        if (name, k) not in rolled:
            rolled[name, k] = pltpu.roll(packed[name], k, 1)
        return rolled[name, k]

    for t in range(3):
        lane_shift = shift_ref[t:t + 1, :]
        cos_t = jnp.ones_like(ang)
        sin_t = jnp.zeros_like(ang)
        for k in sorted(set(int(v) for v in _ROPE_SHIFT[t] if v >= 0)):
            cos_t = jnp.where(lane_shift == k, moved("cos", k), cos_t)
            sin_t = jnp.where(lane_shift == k, moved("sin", k), sin_t)
        cos_ref[t] = cos_t
        sin_ref[t] = sin_t * sign_ref[t:t + 1, :]


def _rope_tables(positions):
    t = positions.size
    tm = TOKEN_TILE
    tab = jax.ShapeDtypeStruct((3, t, LANES), F32)
    return pl.pallas_call(
        _rope_table_kernel,
        grid=(t // tm,),
        in_specs=[pl.BlockSpec((tm, 1), lambda i: (i, 0)),
                  pl.BlockSpec((1, LANES), lambda i: (0, 0)),
                  pl.BlockSpec((3, LANES), lambda i: (0, 0)),
                  pl.BlockSpec((3, LANES), lambda i: (0, 0))],
        out_specs=[pl.BlockSpec((3, tm, LANES), lambda i: (0, i, 0)),
                   pl.BlockSpec((3, tm, LANES), lambda i: (0, i, 0))],
        out_shape=[tab, tab],
        compiler_params=_params("parallel"),
        name="rope_tables",
    )(positions.reshape(t, 1), jnp.asarray(_ROPE_INV), jnp.asarray(_ROPE_SHIFT),
      jnp.asarray(_ROPE_SIGN))


def _wavefront(n_items, stages):
    state = [None] * n_items
    for wave in range(n_items + len(stages) - 1):
        for s in reversed(range(len(stages))):
            n = wave - s
            if 0 <= n < n_items:
                state[n] = stages[s](n, state[n])


def _row_block(n):
    return pl.ds(n * SUB_TILE, SUB_TILE)


def _ffn_kernel(x_ref, g_ref, w13_ref, w2_ref, *rest):
    if len(rest) == 1:
        (o_ref,) = rest
    else:
        src13_ref, src2_ref, o_ref, dst13_ref, dst2_ref = rest
        dst13_ref[...] = src13_ref[...].astype(BF16)
        dst2_ref[...] = src2_ref[...].astype(BF16)

    def pre_norm(n, _):
        return _rms(x_ref[_row_block(n), :], g_ref[0:1, :]).astype(BF16)

    def hidden(n, h):
        gate = _dot(h, w13_ref[:, :D_FF])
        up = _dot(h, w13_ref[:, D_FF:])
        return (gate * jax.nn.sigmoid(gate) * up).astype(BF16)

    def project(n, act):
        y = _dot(act, w2_ref[...])
        o_ref[_row_block(n), :] = x_ref[_row_block(n), :] + 0.5 * _rms(y, g_ref[1:2, :])

    _wavefront(x_ref.shape[0] // SUB_TILE, (pre_norm, hidden, project))


def _resident(shape, layer):
    zeros = (0,) * len(shape)
    return pl.BlockSpec((None,) + tuple(shape), lambda *_: (layer,) + zeros,
                        pipeline_mode=pl.Buffered(1))


def _ffn(x, gains, layer, w13, w2, cast_next=None):
    t = x.shape[0]
    tm = DENSE_TILE
    steps = t // tm
    in_specs = [pl.BlockSpec((tm, D_MODEL), lambda i: (i, 0)),
                _resident((2, D_MODEL), layer),
                _resident((D_MODEL, 2 * D_FF), 0),
                _resident((D_FF, D_MODEL), 0)]
    out_specs = [pl.BlockSpec((tm, D_MODEL), lambda i: (i, 0))]
    out_shape = [jax.ShapeDtypeStruct((t, D_MODEL), F32)]
    operands = [x, gains, w13, w2]
    if cast_next is not None:
        src13, src2, layer_n = cast_next
        for src in (src13, src2):
            rows, cols = src.shape[1] // steps, src.shape[2]
            in_specs.append(pl.BlockSpec((None, rows, cols), lambda i: (layer_n, i, 0)))
            out_specs.append(pl.BlockSpec((None, rows, cols), lambda i: (0, i, 0)))
            out_shape.append(jax.ShapeDtypeStruct((1,) + src.shape[1:], BF16))
            operands.append(src)
    outs = pl.pallas_call(
        _ffn_kernel,
        grid=(steps,),
        in_specs=in_specs,
        out_specs=out_specs,
        out_shape=out_shape,
        compiler_params=_params("parallel"),
        name="ffn",
    )(*operands)
    return outs[0] if cast_next is None else outs


_C_RQ, _C_RK, _C_RV, _C_RG = 0, 256, 512, 1024
_C_MQ, _C_MKV = 1536, 1792
_C_DQ, _C_DK, _C_DV = 2048, 2560, 3072
PROJ_WIDTH = 3584


def _inproj_kernel(x_ref, g_ref, wa_ref, qn_ref, kvn_ref, wqb_ref, wkbk_ref, wkbv_ref,
                   cos_ref, sin_ref,
                   rq_ref, rk_ref, rv_ref, rg_ref, mq_ref, mk_ref, mv_ref,
                   dq_ref, dk_ref, dv_ref):
    h = _rms(x_ref[...], g_ref[0:1, :]).astype(BF16)

    def proj(start, width):
        return _dot_nt(h, wa_ref[start:start + width, :])

    def rope_store(dst, val, table, half, scale):
        cos, sin = cos_ref[table], sin_ref[table]
        for c in range(val.shape[1] // LANES):
            sl = slice(c * LANES, (c + 1) * LANES)
            r = _rope(val[:, sl], cos, sin, half)
            if scale != 1.0:
                r = r * scale
            dst[:, sl] = r.astype(BF16)

    def cast_store(dst):
        def store(val):
            dst[...] = val.astype(BF16)
        return store

    def silu_store(val):
        rg_ref[...] = (val * jax.nn.sigmoid(val)).astype(BF16)

    low_rank = {}

    def keep_cq(val):
        low_rank["cq"] = _rms(val, qn_ref[...]).astype(BF16)

    def keep_kv(val):
        low_rank["ckv"] = _rms(val[:, :MLA_KV_RANK], kvn_ref[...]).astype(BF16)
        low_rank["k_rope"] = _rope(val[:, MLA_KV_RANK:], cos_ref[1], sin_ref[1], MLA_ROPE // 2)

    def key_store(k_nope):
        for hd in range(MLA_HEADS):
            sl = slice(hd * LANES, (hd + 1) * LANES)
            mk_ref[:, sl] = (k_nope[:, sl] + low_rank["k_rope"]).astype(BF16)

    def roped(dst, table, half, scale):
        return lambda val: rope_store(dst, val, table, half, scale)

    items = (
        (lambda: proj(_C_RQ, 256), roped(rq_ref, 0, RET_DK // 2, RET_DK ** -0.5)),
        (lambda: proj(_C_MQ, MLA_Q_RANK), keep_cq),
        (lambda: proj(_C_RK, 256), roped(rk_ref, 0, RET_DK // 2, 1.0)),
        (lambda: proj(_C_MKV, MLA_KV_RANK + LANES), keep_kv),
        (lambda: _dot(low_rank["cq"], wqb_ref[...]),
         roped(mq_ref, 1, MLA_ROPE // 2, LOG2E * (MLA_NOPE + MLA_ROPE) ** -0.5)),
        (lambda: proj(_C_RV, 512), cast_store(rv_ref)),
        (lambda: _dot(low_rank["ckv"], wkbk_ref[...]), key_store),
        (lambda: proj(_C_RG, 512), silu_store),
        (lambda: _dot(low_rank["ckv"], wkbv_ref[...]), cast_store(mv_ref)),
        (lambda: proj(_C_DQ, 512), roped(dq_ref, 2, DIFF_ROT // 2, LOG2E * DIFF_HD ** -0.5)),
        (lambda: proj(_C_DK, 512), roped(dk_ref, 2, DIFF_ROT // 2, 1.0)),
        (lambda: proj(_C_DV, 512), cast_store(dv_ref)),
    )
    _skewed(len(items), lambda n: items[n][0](), lambda n, val: items[n][1](val))


def _inproj(x, gains, wa, q_norm, kv_norm, wqb, wkbk, wkbv, cos_tab, sin_tab, layer):
    t = x.shape[0]
    tm = TOKEN_TILE
    widths = (256, 256, 512, 512, 1024, 1024, 512, 512, 512, 512)

    return pl.pallas_call(
        _inproj_kernel,
        grid=(t // tm,),
        in_specs=[pl.BlockSpec((tm, D_MODEL), lambda i: (i, 0)),
                  _resident((2, D_MODEL), layer),
                  _resident((PROJ_WIDTH, D_MODEL), layer),
                  _resident((1, MLA_Q_RANK), layer),
                  _resident((1, MLA_KV_RANK), layer),
                  _resident((MLA_Q_RANK, MLA_HEADS * LANES), layer),
                  _resident((MLA_KV_RANK, MLA_HEADS * LANES), layer),
                  _resident((MLA_KV_RANK, MLA_HEADS * MLA_DV), layer),
                  pl.BlockSpec((3, tm, LANES), lambda i: (0, i, 0)),
                  pl.BlockSpec((3, tm, LANES), lambda i: (0, i, 0))],
        out_specs=[pl.BlockSpec((tm, w), lambda i: (i, 0)) for w in widths],
        out_shape=[jax.ShapeDtypeStruct((t, w), BF16) for w in widths],
        compiler_params=_params("parallel"),
        name="inproj",
    )(x, gains, wa, q_norm, kv_norm, wqb, wkbk, wkbv, cos_tab, sin_tab)


def _retention_kernel(q_ref, k_ref, v_ref, g_ref, o_ref, state_sc, decay_sc, qk_dec_sc):
    r = RET_TILE
    state_sc[...] = jnp.zeros_like(state_sc)

    row = lax.broadcasted_iota(jnp.int32, (r, r), 0)
    col = lax.broadcasted_iota(jnp.int32, (r, r), 1)
    allowed = (col >> 6) <= (row >> 6)
    dist = jnp.abs(row - col).astype(F32)
    for h in range(RET_HEADS):
        decay_sc[h] = jnp.where(allowed, jnp.exp(RET_LOG_GAMMA[h] * dist), 0.0)
    lane = lax.broadcasted_iota(jnp.int32, (r, LANES), 1)
    first_head = lane < RET_DK
    n_local = lax.broadcasted_iota(jnp.int32, (r, LANES), 0).astype(F32)
    for p in range(RET_HEADS // 2):
        lg_lane = jnp.where(first_head, RET_LOG_GAMMA[2 * p], RET_LOG_GAMMA[2 * p + 1])
        qk_dec_sc[p, 0] = jnp.exp(lg_lane * (n_local + 1.0))
        qk_dec_sc[p, 1] = jnp.exp(lg_lane * (r - 1.0 - n_local))
    s_row = lax.broadcasted_iota(jnp.int32, (LANES, 2 * RET_DV), 0)
    s_col = lax.broadcasted_iota(jnp.int32, (LANES, 2 * RET_DV), 1)
    same_head = (s_row < RET_DK) == (s_col < RET_DV)

    def block(t, carry):
        rows = pl.ds(pl.multiple_of(t * r, r), r)
        for p in range(RET_HEADS // 2):
            lg0, lg1 = RET_LOG_GAMMA[2 * p], RET_LOG_GAMMA[2 * p + 1]
            q = q_ref[rows, p * LANES:(p + 1) * LANES]
            k = k_ref[rows, p * LANES:(p + 1) * LANES]
            v = v_ref[rows, 2 * p * RET_DV:2 * (p + 1) * RET_DV]
            state = state_sc[p]

            q_dec = (q.astype(F32) * qk_dec_sc[p, 0]).astype(BF16)
            o_cross = _dot(q_dec, state.astype(BF16))

            for hh in range(2):
                qm = jnp.where(first_head != bool(hh), q, jnp.zeros_like(q))
                scores = (_dot_nt(qm, k) * decay_sc[2 * p + hh]).astype(BF16)
                sl = slice(hh * RET_DV, (hh + 1) * RET_DV)
                o = _dot(scores, v[:, sl]) + o_cross[:, sl]
                gsl = slice((2 * p + hh) * RET_DV, (2 * p + hh + 1) * RET_DV)
                o_ref[rows, gsl] = (_rms(o) * g_ref[rows, gsl].astype(F32)).astype(BF16)

            k_dec = k.astype(F32) * qk_dec_sc[p, 1]
            kv = _dot(k_dec.T.astype(BF16), v)
            block_decay = jnp.where(s_col < RET_DV, math.exp(lg0 * r), math.exp(lg1 * r))
            state_sc[p] = state * block_decay + jnp.where(same_head, kv, 0.0)
        return carry

    lax.fori_loop(0, q_ref.shape[0] // r, block, 0)


def _retention(q, k, v, g):
    b, s, _ = q.shape
    r = RET_TILE

    def spec(w):
        return pl.BlockSpec((None, s, w), lambda i: (i, 0, 0))

    return pl.pallas_call(
        _retention_kernel,
        grid=(b,),
        in_specs=[spec(256), spec(256), spec(512), spec(512)],
        out_specs=spec(512),
        out_shape=jax.ShapeDtypeStruct((b, s, BRANCH_WIDTH), BF16),
        scratch_shapes=[pltpu.VMEM((RET_HEADS // 2, LANES, 2 * RET_DV), F32),
                        pltpu.VMEM((RET_HEADS, r, r), F32),
                        pltpu.VMEM((RET_HEADS // 2, 2, r, LANES), F32)],
        compiler_params=_params("parallel"),
        name="retention",
    )(q, k, v, g)


ONES_ROWS = 16


def _scores_t(q, k_ref, lo, bias_t):
    tq = q.shape[0]
    s_diag = _dot_nt(k_ref[lo:lo + tq, :], q) + bias_t
    m = jnp.max(s_diag, axis=0, keepdims=True)
    s_past = None
    if lo > 0:
        s_past = _dot_nt(k_ref[0:lo, :], q)
        m = jnp.maximum(m, jnp.max(s_past, axis=0, keepdims=True))
    return s_diag, s_past, m


def _weighted_values_t(scores, vt_ref, lo):
    s_diag, s_past, m = scores
    tq = s_diag.shape[1]
    out = _dot(vt_ref[:, lo:lo + tq], jnp.exp2(s_diag - m).astype(BF16))
    for t in range(lo // tq):
        keys = slice(t * tq, (t + 1) * tq)
        out = out + _dot(vt_ref[:, keys], jnp.exp2(s_past[keys, :] - m).astype(BF16))
    return out


def _skewed(n_items, first_stage, second_stage):
    pending = first_stage(0)
    for n in range(n_items):
        ahead = first_stage(n + 1) if n + 1 < n_items else None
        second_stage(n, pending)
        pending = ahead


def _chunk_bias_t(n):
    key = lax.broadcasted_iota(jnp.int32, (n, n), 0)
    qry = lax.broadcasted_iota(jnp.int32, (n, n), 1)
    return jnp.where((key >> 6) <= (qry >> 6), 0.0, NEG_INF).astype(F32)


def _mla_attn_kernel(q_ref, k_ref, v_ref, o_ref, vt_sc):
    tq = ATTN_TILE
    s = q_ref.shape[0]
    vt = v_ref[...].astype(F32).T
    for hh in range(2):
        vt_sc[hh, 0:MLA_DV, :] = vt[hh * MLA_DV:(hh + 1) * MLA_DV, :].astype(BF16)
        vt_sc[hh, MLA_DV:, :] = jnp.ones((ONES_ROWS, s), BF16)
    bias_t = _chunk_bias_t(tq)

    def scores(i):
        lo = i * tq
        return [_scores_t(q_ref[lo:lo + tq, hh * LANES:(hh + 1) * LANES],
                          k_ref.at[:, hh * LANES:(hh + 1) * LANES], lo, bias_t) for hh in range(2)]

    def outputs(i, sc):
        lo = i * tq
        accs = [_weighted_values_t(sc[hh], vt_sc.at[hh], lo) for hh in range(2)]
        halves = [acc[0:MLA_DV, :] / acc[MLA_DV:MLA_DV + 1, :] for acc in accs]
        o_ref[lo:lo + tq, :] = jnp.concatenate(halves, axis=0).T.astype(BF16)

    _skewed(s // tq, scores, outputs)


def _mla_attention(q, k, v):
    b, s, _ = q.shape
    pairs = MLA_HEADS // 2
    return pl.pallas_call(
        _mla_attn_kernel,
        grid=(b, pairs),
        in_specs=[pl.BlockSpec((None, s, 2 * LANES), lambda i, j: (i, 0, j)),
                  pl.BlockSpec((None, s, 2 * LANES), lambda i, j: (i, 0, j)),
                  pl.BlockSpec((None, s, LANES), lambda i, j: (i, 0, j))],
        out_specs=pl.BlockSpec((None, s, LANES), lambda i, j: (i, 0, j)),
        out_shape=jax.ShapeDtypeStruct((b, s, BRANCH_WIDTH), BF16),
        scratch_shapes=[pltpu.VMEM((2, MLA_DV + ONES_ROWS, s), BF16)],
        compiler_params=_params("parallel", "parallel"),
        name="mla_attention",
    )(q, k, v)


def _diff_attn_kernel(q_ref, k_ref, v_ref, lam_ref, linit_ref, o_ref, vt_sc):
    tq = ATTN_TILE
    s = q_ref.shape[0]
    dv = 2 * DIFF_HD
    vt_sc[0:dv, :] = v_ref[...].astype(F32).T.astype(BF16)
    vt_sc[dv:, :] = jnp.ones((ONES_ROWS, s), BF16)
    bias_t = _chunk_bias_t(tq)
    lane = lax.broadcasted_iota(jnp.int32, (tq, LANES), 1)
    lp = lam_ref[...]
    linit = linit_ref[...]
    lam = (jnp.exp(jnp.sum(lp[0:1] * lp[1:2], axis=-1, keepdims=True))
           - jnp.exp(jnp.sum(lp[2:3] * lp[3:4], axis=-1, keepdims=True)) + linit)
    def scores(i):
        lo = i * tq
        q = q_ref[lo:lo + tq, :]
        zero = jnp.zeros_like(q)
        return [_scores_t(jnp.where((lane < DIFF_HD) != bool(j), q, zero), k_ref, lo, bias_t)
                for j in range(2)]

    def outputs(i, sc):
        lo = i * tq
        accs = [_weighted_values_t(sc[j], vt_sc, lo) for j in range(2)]
        maps = [acc[0:dv, :] / acc[dv:dv + 1, :] for acc in accs]
        o_t = maps[0] - lam * maps[1]
        o_t = o_t * lax.rsqrt(jnp.mean(o_t * o_t, axis=0, keepdims=True) + EPS)
        o_ref[lo:lo + tq, :] = (o_t * (1.0 - linit)).T.astype(BF16)

    _skewed(s // tq, scores, outputs)


def _diff_attention(q, k, v, lam_params, lambda_init):
    b, s, _ = q.shape

    def spec():
        return pl.BlockSpec((None, s, LANES), lambda i, j: (i, 0, j))

    return pl.pallas_call(
        _diff_attn_kernel,
        grid=(b, DIFF_HEADS),
        in_specs=[spec(), spec(), spec(),
                  pl.BlockSpec((4, DIFF_HD), lambda i, j: (0, 0)),
                  pl.BlockSpec((1, 1), lambda i, j: (0, 0))],
        out_specs=spec(),
        out_shape=jax.ShapeDtypeStruct((b, s, BRANCH_WIDTH), BF16),
        scratch_shapes=[pltpu.VMEM((2 * DIFF_HD + ONES_ROWS, s), BF16)],
        compiler_params=_params("parallel", "parallel"),
        name="diff_attention",
    )(q, k, v, lam_params, jnp.full((1, 1), lambda_init, F32))


def _merge_kernel(x_ref, g_ref, yr_ref, ym_ref, yd_ref, wg_ref, wb_ref, wo_ref, o_ref):
    def pre_norm(n, _):
        return _rms(x_ref[_row_block(n), :], g_ref[0:1, :]).astype(BF16)

    def gated_sum(n, h):
        mixed = None
        for br, y_ref in enumerate((yr_ref, ym_ref, yd_ref)):
            gate = jax.nn.sigmoid(_dot_nt(h, wg_ref[br * D_MODEL:(br + 1) * D_MODEL, :]))
            term = gate * _dot(y_ref[_row_block(n), :], wb_ref[br])
            mixed = term if mixed is None else mixed + term
        return mixed.astype(BF16)

    def project(n, mixed):
        y = _dot(mixed, wo_ref[...])
        o_ref[_row_block(n), :] = x_ref[_row_block(n), :] + _rms(y, g_ref[1:2, :])

    _wavefront(x_ref.shape[0] // SUB_TILE, (pre_norm, gated_sum, project))


def _merge(x, gains, y_ret, y_mla, y_diff, wg, wb, wo, layer):
    t = x.shape[0]
    tm = DENSE_TILE

    def rows(w):
        return pl.BlockSpec((tm, w), lambda i: (i, 0))

    return pl.pallas_call(
        _merge_kernel,
        grid=(t // tm,),
        in_specs=[rows(D_MODEL),
                  _resident((2, D_MODEL), layer),
                  rows(BRANCH_WIDTH), rows(BRANCH_WIDTH), rows(BRANCH_WIDTH),
                  _resident((N_BRANCH * D_MODEL, D_MODEL), layer),
                  _resident((N_BRANCH, BRANCH_WIDTH, D_MODEL), layer),
                  _resident((D_MODEL, D_MODEL), layer)],
        out_specs=rows(D_MODEL),
        out_shape=jax.ShapeDtypeStruct((t, D_MODEL), F32),
        compiler_params=_params("parallel"),
        name="merge",
    )(x, gains, y_ret, y_mla, y_diff, wg, wb, wo)


def _memkv_kernel(mem_ref, g_ref, w_ref, o_ref):
    m = _rms(mem_ref[...], g_ref[2:3, :]).astype(BF16)
    o_ref[...] = _dot(m, w_ref[...]).astype(BF16)


def _memkv(mem, gains, wkv):
    b, n, _ = mem.shape
    depth = wkv.shape[0]
    return pl.pallas_call(
        _memkv_kernel,
        grid=(depth, b),
        in_specs=[pl.BlockSpec((None, n, D_MODEL), lambda l, i: (i, 0, 0)),
                  pl.BlockSpec((None, 3, D_MODEL), lambda l, i: (l, 0, 0)),
                  pl.BlockSpec((None, D_MODEL, 2 * D_MODEL), lambda l, i: (l, 0, 0))],
        out_specs=pl.BlockSpec((None, None, n, 2 * D_MODEL), lambda l, i: (l, i, 0, 0)),
        out_shape=jax.ShapeDtypeStruct((depth, b, n, 2 * D_MODEL), BF16),
        compiler_params=_params("parallel", "parallel"),
        name="memkv",
    )(mem, gains, wkv)


def _cross_kernel(x_ref, g_ref, kv_ref, wq_ref, wo_ref, o_ref):
    def query(n, _):
        h = _rms(x_ref[_row_block(n), :], g_ref[0:1, :]).astype(BF16)
        return (_dot(h, wq_ref[...]) * (LOG2E * CROSS_HD ** -0.5)).astype(BF16)

    def attend(n, q):
        heads = []
        for hd in range(CROSS_HEADS):
            cols = slice(hd * CROSS_HD, (hd + 1) * CROSS_HD)
            vcols = slice(D_MODEL + hd * CROSS_HD, D_MODEL + (hd + 1) * CROSS_HD)
            sc = _dot_nt(q[:, cols], kv_ref[:, cols])
            p = jnp.exp2(sc - jnp.max(sc, axis=-1, keepdims=True))
            total = jnp.sum(p, axis=-1, keepdims=True)
            heads.append((_dot(p.astype(BF16), kv_ref[:, vcols]) / total).astype(BF16))
        return jnp.concatenate(heads, axis=-1)

    def project(n, o):
        y = _dot(o, wo_ref[...])
        o_ref[_row_block(n), :] = x_ref[_row_block(n), :] + _rms(y, g_ref[1:2, :])

    _wavefront(x_ref.shape[0] // SUB_TILE, (query, attend, project))


def _cross(x, gains, kv, wq, wo, layer):
    b, s, _ = x.shape
    tm = DENSE_TILE
    n = kv.shape[2]
    return pl.pallas_call(
        _cross_kernel,
        grid=(b, s // tm),
        in_specs=[pl.BlockSpec((None, tm, D_MODEL), lambda i, j: (i, j, 0)),
                  _resident((3, D_MODEL), layer),
                  pl.BlockSpec((None, None, n, 2 * D_MODEL), lambda i, j: (layer, i, 0, 0)),
                  _resident((D_MODEL, D_MODEL), layer),
                  _resident((D_MODEL, D_MODEL), layer)],
        out_specs=pl.BlockSpec((None, tm, D_MODEL), lambda i, j: (i, j, 0)),
        out_shape=jax.ShapeDtypeStruct((b, s, D_MODEL), F32),
        compiler_params=_params("parallel", "parallel"),
        name="cross_attention",
    )(x, gains, kv, wq, wo)


K_ROPE_AT = 1920


def _pack_inproj(w_in, wq_b, wkv_b):
    depth, d, _ = w_in.shape
    w_t = jnp.swapaxes(w_in, 1, 2)
    wa = jnp.concatenate(
        [w_t[:, :K_ROPE_AT].astype(BF16),
         jnp.zeros((depth, MLA_NOPE, d), BF16),
         w_t[:, K_ROPE_AT:K_ROPE_AT + MLA_ROPE].astype(BF16),
         jnp.zeros((depth, LANES - MLA_NOPE - MLA_ROPE, d), BF16),
         w_t[:, K_ROPE_AT + MLA_ROPE:GATE_OFFSET].astype(BF16)], axis=1)
    wg = w_t[:, GATE_OFFSET:].astype(BF16)
    wq = wq_b.astype(BF16).reshape(depth, MLA_Q_RANK, MLA_HEADS, MLA_NOPE + MLA_ROPE)
    wq = jnp.pad(wq, ((0, 0), (0, 0), (0, 0), (0, LANES - MLA_NOPE - MLA_ROPE)))
    wq = wq.reshape(depth, MLA_Q_RANK, MLA_HEADS * LANES)
    wkv = wkv_b.astype(BF16).reshape(depth, MLA_KV_RANK, MLA_HEADS, MLA_NOPE + MLA_DV)
    wk = jnp.pad(wkv[..., :MLA_NOPE], ((0, 0), (0, 0), (0, 0), (0, LANES - MLA_NOPE)))
    wk = wk.reshape(depth, MLA_KV_RANK, MLA_HEADS * LANES)
    wv = wkv[..., MLA_NOPE:].reshape(depth, MLA_KV_RANK, MLA_HEADS * MLA_DV)
    return wa, wg, wq, wk, wv


def kernel(x, mem, positions, ffn1_norms, ffn1_w13, ffn1_w2, mix_norms, w_in, mla_q_norm,
           mla_kv_norm, mla_wq_b, mla_wkv_b, diff_lambda, w_branch, w_out, cross_norms,
           cross_wq, cross_wkv, cross_wo, ffn2_norms, ffn2_w13, ffn2_w2):
    b, s, d = x.shape
    t = b * s
    depth = w_in.shape[0]
    cos_tab, sin_tab = _rope_tables(positions)

    f1_w13, f1_w2 = ffn1_w13[0:1].astype(BF16), ffn1_w2[0:1].astype(BF16)
    wa, wg, wqb, wkbk, wkbv = _pack_inproj(w_in, mla_wq_b, mla_wkv_b)
    wb, wo = w_branch.astype(BF16), w_out.astype(BF16)
    c_wq, c_wo = cross_wq.astype(BF16), cross_wo.astype(BF16)
    q_norm = mla_q_norm.reshape(depth, 1, MLA_Q_RANK)
    kv_norm = mla_kv_norm.reshape(depth, 1, MLA_KV_RANK)

    mem_kv = _memkv(mem, cross_norms, cross_wkv.astype(BF16))

    def seq(a):
        return a.reshape(b, s, a.shape[-1])

    xf = x.reshape(t, d)
    for l in range(depth):
        lambda_init = 0.8 - 0.6 * math.exp(-0.3 * l)
        xf, f2_w13, f2_w2 = _ffn(xf, ffn1_norms, l, f1_w13, f1_w2,
                                 cast_next=(ffn2_w13, ffn2_w2, l))

        (rq, rk, rv, rg, mq, mk, mv, dq, dk, dv) = _inproj(
            xf, mix_norms, wa, q_norm, kv_norm, wqb, wkbk, wkbv, cos_tab, sin_tab, l)
        y_ret = _retention(seq(rq), seq(rk), seq(rv), seq(rg))
        y_mla = _mla_attention(seq(mq), seq(mk), seq(mv))
        y_diff = _diff_attention(seq(dq), seq(dk), seq(dv), diff_lambda[l], lambda_init)
        xf = _merge(xf, mix_norms, y_ret.reshape(t, -1), y_mla.reshape(t, -1),
                    y_diff.reshape(t, -1), wg, wb, wo, l)

        xf = _cross(xf.reshape(b, s, d), cross_norms, mem_kv, c_wq, c_wo, l).reshape(t, d)

        if l + 1 < depth:
            xf, f1_w13, f1_w2 = _ffn(xf, ffn2_norms, l, f2_w13, f2_w2,
                                     cast_next=(ffn1_w13, ffn1_w2, l + 1))
        else:
            xf = _ffn(xf, ffn2_norms, l, f2_w13, f2_w2)
    return xf.reshape(b, s, d)
```

```python
import functools
import math

import numpy as np
import jax
import jax.numpy as jnp
from jax import lax
from jax.experimental import pallas as pl
from jax.experimental.pallas import tpu as pltpu

F32 = jnp.float32
BF16 = jnp.bfloat16

D_MODEL = 1024
DEPTH = 4
CHUNK = 64
EPS = 1e-6
NEG_INF = -1e30
ROPE_THETA = 500000.0
RET_THETA = 10000.0

RET_HEADS = 4
RET_DK = 64
RET_DV = 128
MLA_HEADS = 8
MLA_Q_RANK = 256
MLA_KV_RANK = 128
MLA_NOPE = 64
MLA_ROPE = 32
MLA_DV = 64
DIFF_HEADS = 4
DIFF_HD = 64
DIFF_ROT = 16
N_BRANCH = 3
BRANCH_WIDTH = 512
CROSS_HEADS = 4
CROSS_HD = 256
D_FF = 2816
GATE_OFFSET = 3488

LANES = 128
VMEM_LIMIT = 56 * 1024 * 1024

TOKEN_TILE = 512
DENSE_TILE = 1024
SUB_TILE = 256
ATTN_TILE = 256
RET_TILE = 256

LOG2E = math.log2(math.e)
RET_LOG_GAMMA =tuple(math.log(1.0 - 2.0 ** (-5.0 - h)) for h in range(RET_HEADS))


def _params(*sem):
    return pltpu.CompilerParams(dimension_semantics=sem, vmem_limit_bytes=VMEM_LIMIT)


def _rms(x, gain=None):
    y = x * lax.rsqrt(jnp.mean(x * x, axis=-1, keepdims=True) + EPS)
    return y if gain is None else y * gain


def _dot(a, b):
    return jnp.dot(a, b, preferred_element_type=F32)


def _dot_nt(a, b):
    return lax.dot_general(a, b, (((1,), (1,)), ((), ())), preferred_element_type=F32)


def _rope(x, cos, sin_signed, half):
    lane = lax.broadcasted_iota(jnp.int32, x.shape, 1)
    first = (lane & (2 * half - 1)) < half
    partner = jnp.where(first, pltpu.roll(x, LANES - half, 1), pltpu.roll(x, half, 1))
    return x * cos + partner * sin_signed


def _rope_patterns():
    inv = np.zeros((1, LANES), np.float32)
    inv[0, 0:32] = 1.0 / (RET_THETA ** (np.arange(0, RET_DK, 2, dtype=np.float32) / RET_DK))
    inv[0, 32:48] = 1.0 / (ROPE_THETA ** (np.arange(0, MLA_ROPE, 2, dtype=np.float32) / MLA_ROPE))
    inv[0, 48:56] = 1.0 / (ROPE_THETA ** (np.arange(0, DIFF_ROT, 2, dtype=np.float32) / DIFF_ROT))
    shift = np.full((3, LANES), -1, np.int32)
    sign = np.zeros((3, LANES), np.float32)
    for i in range(LANES):
        j = i % RET_DK
        shift[0, i] = (i - j % 32) % LANES
        sign[0, i] = -1.0 if j < 32 else 1.0
        if 64 <= i < 96:
            j = i - 64
            shift[1, i] = (i - (32 + j % 16)) % LANES
            sign[1, i] = -1.0 if j < 16 else 1.0
        j = i % DIFF_HD
        if j < DIFF_ROT:
            shift[2, i] = (i - (48 + j % 8)) % LANES
            sign[2, i] = -1.0 if j < 8 else 1.0
    return inv, shift, sign


_ROPE_INV, _ROPE_SHIFT, _ROPE_SIGN = _rope_patterns()


def _rope_table_kernel(pos_ref, inv_ref, shift_ref, sign_ref, cos_ref, sin_ref):
    ang = pos_ref[...].astype(F32) * inv_ref[...]
    packed = {"cos": jnp.cos(ang), "sin": jnp.sin(ang)}
    rolled = {}

    def moved(name, k):
        if k == 0:
            return packed[name]
        if (name, k) not in rolled:
            rolled[name, k] = pltpu.roll(packed[name], k, 1)
        return rolled[name, k]

    for t in range(3):
        lane_shift = shift_ref[t:t + 1, :]
        cos_t = jnp.ones_like(ang)
        sin_t = jnp.zeros_like(ang)
        for k in sorted(set(int(v) for v in _ROPE_SHIFT[t] if v >= 0)):
            cos_t = jnp.where(lane_shift == k, moved("cos", k), cos_t)
            sin_t = jnp.where(lane_shift == k, moved("sin", k), sin_t)
        cos_ref[t] = cos_t
        sin_ref[t] = sin_t * sign_ref[t:t + 1, :]


def _rope_tables(positions):
    t = positions.size
    tm = TOKEN_TILE
    tab = jax.ShapeDtypeStruct((3, t, LANES), F32)
    return pl.pallas_call(
        _rope_table_kernel,
        grid=(t // tm,),
        in_specs=[pl.BlockSpec((tm, 1), lambda i: (i, 0)),
                  pl.BlockSpec((1, LANES), lambda i: (0, 0)),
                  pl.BlockSpec((3, LANES), lambda i: (0, 0)),
                  pl.BlockSpec((3, LANES), lambda i: (0, 0))],
        out_specs=[pl.BlockSpec((3, tm, LANES), lambda i: (0, i, 0)),
                   pl.BlockSpec((3, tm, LANES), lambda i: (0, i, 0))],
        out_shape=[tab, tab],
        compiler_params=_params("parallel"),
        name="rope_tables",
    )(positions.reshape(t, 1), jnp.asarray(_ROPE_INV), jnp.asarray(_ROPE_SHIFT),
      jnp.asarray(_ROPE_SIGN))


def _wavefront(n_items, stages):
    state = [None] * n_items
    for wave in range(n_items + len(stages) - 1):
        for s in reversed(range(len(stages))):
            n = wave - s
            if 0 <= n < n_items:
                state[n] = stages[s](n, state[n])


def _row_block(n):
    return pl.ds(n * SUB_TILE, SUB_TILE)


def _ffn_kernel(x_ref, g_ref, w13_ref, w2_ref, *rest):
    if len(rest) == 1:
        (o_ref,) = rest
    else:
        src13_ref, src2_ref, o_ref, dst13_ref, dst2_ref = rest
        dst13_ref[...] = src13_ref[...].astype(BF16)
        dst2_ref[...] = src2_ref[...].astype(BF16)

    def pre_norm(n, _):
        return _rms(x_ref[_row_block(n), :], g_ref[0:1, :]).astype(BF16)

    def hidden(n, h):
        gate = _dot(h, w13_ref[:, :D_FF])
        up = _dot(h, w13_ref[:, D_FF:])
        return (gate * jax.nn.sigmoid(gate) * up).astype(BF16)

    def project(n, act):
        y = _dot(act, w2_ref[...])
        o_ref[_row_block(n), :] = x_ref[_row_block(n), :] + 0.5 * _rms(y, g_ref[1:2, :])

    _wavefront(x_ref.shape[0] // SUB_TILE, (pre_norm, hidden, project))


def _resident(shape, layer):
    zeros = (0,) * len(shape)
    return pl.BlockSpec((None,) + tuple(shape), lambda *_: (layer,) + zeros,
                        pipeline_mode=pl.Buffered(1))


def _ffn(x, gains, layer, w13, w2, cast_next=None):
    t = x.shape[0]
    tm = DENSE_TILE
    steps = t // tm
    in_specs = [pl.BlockSpec((tm, D_MODEL), lambda i: (i, 0)),
                _resident((2, D_MODEL), layer),
                _resident((D_MODEL, 2 * D_FF), 0),
                _resident((D_FF, D_MODEL), 0)]
    out_specs = [pl.BlockSpec((tm, D_MODEL), lambda i: (i, 0))]
    out_shape = [jax.ShapeDtypeStruct((t, D_MODEL), F32)]
    operands = [x, gains, w13, w2]
    if cast_next is not None:
        src13, src2, layer_n = cast_next
        for src in (src13, src2):
            rows, cols = src.shape[1] // steps, src.shape[2]
            in_specs.append(pl.BlockSpec((None, rows, cols), lambda i: (layer_n, i, 0)))
            out_specs.append(pl.BlockSpec((None, rows, cols), lambda i: (0, i, 0)))
            out_shape.append(jax.ShapeDtypeStruct((1,) + src.shape[1:], BF16))
            operands.append(src)
    outs = pl.pallas_call(
        _ffn_kernel,
        grid=(steps,),
        in_specs=in_specs,
        out_specs=out_specs,
        out_shape=out_shape,
        compiler_params=_params("parallel"),
        name="ffn",
    )(*operands)
    return outs[0] if cast_next is None else outs


_C_RQ, _C_RK, _C_RV, _C_RG = 0, 256, 512, 1024
_C_MQ, _C_MKV = 1536, 1792
_C_DQ, _C_DK, _C_DV = 2048, 2560, 3072
PROJ_WIDTH = 3584


def _inproj_kernel(x_ref, g_ref, wa_ref, qn_ref, kvn_ref, wqb_ref, wkbk_ref, wkbv_ref,
                   cos_ref, sin_ref,
                   rq_ref, rk_ref, rv_ref, rg_ref, mq_ref, mk_ref, mv_ref,
                   dq_ref, dk_ref, dv_ref):
    h = _rms(x_ref[...], g_ref[0:1, :]).astype(BF16)

    def proj(start, width):
        return _dot_nt(h, wa_ref[start:start + width, :])

    def rope_store(dst, val, table, half, scale):
        cos, sin = cos_ref[table], sin_ref[table]
        for c in range(val.shape[1] // LANES):
            sl = slice(c * LANES, (c + 1) * LANES)
            r = _rope(val[:, sl], cos, sin, half)
            if scale != 1.0:
                r = r * scale
            dst[:, sl] = r.astype(BF16)

    def cast_store(dst):
        def store(val):
            dst[...] = val.astype(BF16)
        return store

    def silu_store(val):
        rg_ref[...] = (val * jax.nn.sigmoid(val)).astype(BF16)

    low_rank = {}

    def keep_cq(val):
        low_rank["cq"] = _rms(val, qn_ref[...]).astype(BF16)

    def keep_kv(val):
        low_rank["ckv"] = _rms(val[:, :MLA_KV_RANK], kvn_ref[...]).astype(BF16)
        low_rank["k_rope"] = _rope(val[:, MLA_KV_RANK:], cos_ref[1], sin_ref[1], MLA_ROPE // 2)

    def key_store(k_nope):
        for hd in range(MLA_HEADS):
            sl = slice(hd * LANES, (hd + 1) * LANES)
            mk_ref[:, sl] = (k_nope[:, sl] + low_rank["k_rope"]).astype(BF16)

    def roped(dst, table, half, scale):
        return lambda val: rope_store(dst, val, table, half, scale)

    items = (
        (lambda: proj(_C_RQ, 256), roped(rq_ref, 0, RET_DK // 2, RET_DK ** -0.5)),
        (lambda: proj(_C_MQ, MLA_Q_RANK), keep_cq),
        (lambda: proj(_C_RK, 256), roped(rk_ref, 0, RET_DK // 2, 1.0)),
        (lambda: proj(_C_MKV, MLA_KV_RANK + LANES), keep_kv),
        (lambda: _dot(low_rank["cq"], wqb_ref[...]),
         roped(mq_ref, 1, MLA_ROPE // 2, LOG2E * (MLA_NOPE + MLA_ROPE) ** -0.5)),
        (lambda: proj(_C_RV, 512), cast_store(rv_ref)),
        (lambda: _dot(low_rank["ckv"], wkbk_ref[...]), key_store),
        (lambda: proj(_C_RG, 512), silu_store),
        (lambda: _dot(low_rank["ckv"], wkbv_ref[...]), cast_store(mv_ref)),
        (lambda: proj(_C_DQ, 512), roped(dq_ref, 2, DIFF_ROT // 2, LOG2E * DIFF_HD ** -0.5)),
        (lambda: proj(_C_DK, 512), roped(dk_ref, 2, DIFF_ROT // 2, 1.0)),
        (lambda: proj(_C_DV, 512), cast_store(dv_ref)),
    )
    _skewed(len(items), lambda n: items[n][0](), lambda n, val: items[n][1](val))


def _inproj(x, gains, wa, q_norm, kv_norm, wqb, wkbk, wkbv, cos_tab, sin_tab, layer):
    t = x.shape[0]
    tm = TOKEN_TILE
    widths = (256, 256, 512, 512, 1024, 1024, 512, 512, 512, 512)

    return pl.pallas_call(
        _inproj_kernel,
        grid=(t // tm,),
        in_specs=[pl.BlockSpec((tm, D_MODEL), lambda i: (i, 0)),
                  _resident((2, D_MODEL), layer),
                  _resident((PROJ_WIDTH, D_MODEL), layer),
                  _resident((1, MLA_Q_RANK), layer),
                  _resident((1, MLA_KV_RANK), layer),
                  _resident((MLA_Q_RANK, MLA_HEADS * LANES), layer),
                  _resident((MLA_KV_RANK, MLA_HEADS * LANES), layer),
                  _resident((MLA_KV_RANK, MLA_HEADS * MLA_DV), layer),
                  pl.BlockSpec((3, tm, LANES), lambda i: (0, i, 0)),
                  pl.BlockSpec((3, tm, LANES), lambda i: (0, i, 0))],
        out_specs=[pl.BlockSpec((tm, w), lambda i: (i, 0)) for w in widths],
        out_shape=[jax.ShapeDtypeStruct((t, w), BF16) for w in widths],
        compiler_params=_params("parallel"),
        name="inproj",
    )(x, gains, wa, q_norm, kv_norm, wqb, wkbk, wkbv, cos_tab, sin_tab)


def _retention_kernel(q_ref, k_ref, v_ref, g_ref, o_ref, state_sc, decay_sc, qk_dec_sc):
    r = RET_TILE
    state_sc[...] = jnp.zeros_like(state_sc)

    row = lax.broadcasted_iota(jnp.int32, (r, r), 0)
    col = lax.broadcasted_iota(jnp.int32, (r, r), 1)
    allowed = (col >> 6) <= (row >> 6)
    dist = jnp.abs(row - col).astype(F32)
    for h in range(RET_HEADS):
        decay_sc[h] = jnp.where(allowed, jnp.exp(RET_LOG_GAMMA[h] * dist), 0.0)
    lane = lax.broadcasted_iota(jnp.int32, (r, LANES), 1)
    first_head = lane < RET_DK
    n_local = lax.broadcasted_iota(jnp.int32, (r, LANES), 0).astype(F32)
    for p in range(RET_HEADS // 2):
        lg_lane = jnp.where(first_head, RET_LOG_GAMMA[2 * p], RET_LOG_GAMMA[2 * p + 1])
        qk_dec_sc[p, 0] = jnp.exp(lg_lane * (n_local + 1.0))
        qk_dec_sc[p, 1] = jnp.exp(lg_lane * (r - 1.0 - n_local))
    s_row = lax.broadcasted_iota(jnp.int32, (LANES, 2 * RET_DV), 0)
    s_col = lax.broadcasted_iota(jnp.int32, (LANES, 2 * RET_DV), 1)
    same_head = (s_row < RET_DK) == (s_col < RET_DV)

    def block(t, carry):
        rows = pl.ds(pl.multiple_of(t * r, r), r)
        for p in range(RET_HEADS // 2):
            lg0, lg1 = RET_LOG_GAMMA[2 * p], RET_LOG_GAMMA[2 * p + 1]
            q = q_ref[rows, p * LANES:(p + 1) * LANES]
            k = k_ref[rows, p * LANES:(p + 1) * LANES]
            v = v_ref[rows, 2 * p * RET_DV:2 * (p + 1) * RET_DV]
            state = state_sc[p]

            q_dec = (q.astype(F32) * qk_dec_sc[p, 0]).astype(BF16)
            o_cross = _dot(q_dec, state.astype(BF16))

            for hh in range(2):
                qm = jnp.where(first_head != bool(hh), q, jnp.zeros_like(q))
                scores = (_dot_nt(qm, k) * decay_sc[2 * p + hh]).astype(BF16)
                sl = slice(hh * RET_DV, (hh + 1) * RET_DV)
                o = _dot(scores, v[:, sl]) + o_cross[:, sl]
                gsl = slice((2 * p + hh) * RET_DV, (2 * p + hh + 1) * RET_DV)
                o_ref[rows, gsl] = (_rms(o) * g_ref[rows, gsl].astype(F32)).astype(BF16)

            k_dec = k.astype(F32) * qk_dec_sc[p, 1]
            kv = _dot(k_dec.T.astype(BF16), v)
            block_decay = jnp.where(s_col < RET_DV, math.exp(lg0 * r), math.exp(lg1 * r))
            state_sc[p] = state * block_decay + jnp.where(same_head, kv, 0.0)
        return carry

    lax.fori_loop(0, q_ref.shape[0] // r, block, 0)


def _retention(q, k, v, g):
    b, s, _ = q.shape
    r = RET_TILE

    def spec(w):
        return pl.BlockSpec((None, s, w), lambda i: (i, 0, 0))

    return pl.pallas_call(
        _retention_kernel,
        grid=(b,),
        in_specs=[spec(256), spec(256), spec(512), spec(512)],
        out_specs=spec(512),
        out_shape=jax.ShapeDtypeStruct((b, s, BRANCH_WIDTH), BF16),
        scratch_shapes=[pltpu.VMEM((RET_HEADS // 2, LANES, 2 * RET_DV), F32),
                        pltpu.VMEM((RET_HEADS, r, r), F32),
                        pltpu.VMEM((RET_HEADS // 2, 2, r, LANES), F32)],
        compiler_params=_params("parallel"),
        name="retention",
    )(q, k, v, g)


ONES_ROWS = 16


def _scores_t(q, k_ref, lo, bias_t):
    tq = q.shape[0]
    s_diag = _dot_nt(k_ref[lo:lo + tq, :], q) + bias_t
    m = jnp.max(s_diag, axis=0, keepdims=True)
    s_past = None
    if lo > 0:
        s_past = _dot_nt(k_ref[0:lo, :], q)
        m = jnp.maximum(m, jnp.max(s_past, axis=0, keepdims=True))
    return s_diag, s_past, m


def _weighted_values_t(scores, vt_ref, lo):
    s_diag, s_past, m = scores
    tq = s_diag.shape[1]
    out = _dot(vt_ref[:, lo:lo + tq], jnp.exp2(s_diag - m).astype(BF16))
    for t in range(lo // tq):
        keys = slice(t * tq, (t + 1) * tq)
        out = out + _dot(vt_ref[:, keys], jnp.exp2(s_past[keys, :] - m).astype(BF16))
    return out


def _skewed(n_items, first_stage, second_stage):
    pending = first_stage(0)
    for n in range(n_items):
        ahead = first_stage(n + 1) if n + 1 < n_items else None
        second_stage(n, pending)
        pending = ahead


def _chunk_bias_t(n):
    key = lax.broadcasted_iota(jnp.int32, (n, n), 0)
    qry = lax.broadcasted_iota(jnp.int32, (n, n), 1)
    return jnp.where((key >> 6) <= (qry >> 6), 0.0, NEG_INF).astype(F32)


def _mla_attn_kernel(q_ref, k_ref, v_ref, o_ref, vt_sc):
    tq = ATTN_TILE
    s = q_ref.shape[0]
    vt = v_ref[...].astype(F32).T
    for hh in range(2):
        vt_sc[hh, 0:MLA_DV, :] = vt[hh * MLA_DV:(hh + 1) * MLA_DV, :].astype(BF16)
        vt_sc[hh, MLA_DV:, :] = jnp.ones((ONES_ROWS, s), BF16)
    bias_t = _chunk_bias_t(tq)

    def scores(i):
        lo = i * tq
        return [_scores_t(q_ref[lo:lo + tq, hh * LANES:(hh + 1) * LANES],
                          k_ref.at[:, hh * LANES:(hh + 1) * LANES], lo, bias_t) for hh in range(2)]

    def outputs(i, sc):
        lo = i * tq
        accs = [_weighted_values_t(sc[hh], vt_sc.at[hh], lo) for hh in range(2)]
        halves = [acc[0:MLA_DV, :] / acc[MLA_DV:MLA_DV + 1, :] for acc in accs]
        o_ref[lo:lo + tq, :] = jnp.concatenate(halves, axis=0).T.astype(BF16)

    _skewed(s // tq, scores, outputs)


def _mla_attention(q, k, v):
    b, s, _ = q.shape
    pairs = MLA_HEADS // 2
    return pl.pallas_call(
        _mla_attn_kernel,
        grid=(b, pairs),
        in_specs=[pl.BlockSpec((None, s, 2 * LANES), lambda i, j: (i, 0, j)),
                  pl.BlockSpec((None, s, 2 * LANES), lambda i, j: (i, 0, j)),
                  pl.BlockSpec((None, s, LANES), lambda i, j: (i, 0, j))],
        out_specs=pl.BlockSpec((None, s, LANES), lambda i, j: (i, 0, j)),
        out_shape=jax.ShapeDtypeStruct((b, s, BRANCH_WIDTH), BF16),
        scratch_shapes=[pltpu.VMEM((2, MLA_DV + ONES_ROWS, s), BF16)],
        compiler_params=_params("parallel", "parallel"),
        name="mla_attention",
    )(q, k, v)


def _diff_attn_kernel(q_ref, k_ref, v_ref, lam_ref, linit_ref, o_ref, vt_sc):
    tq = ATTN_TILE
    s = q_ref.shape[0]
    dv = 2 * DIFF_HD
    vt_sc[0:dv, :] = v_ref[...].astype(F32).T.astype(BF16)
    vt_sc[dv:, :] = jnp.ones((ONES_ROWS, s), BF16)
    bias_t = _chunk_bias_t(tq)
    lane = lax.broadcasted_iota(jnp.int32, (tq, LANES), 1)
    lp = lam_ref[...]
    linit = linit_ref[...]
    lam = (jnp.exp(jnp.sum(lp[0:1] * lp[1:2], axis=-1, keepdims=True))
           - jnp.exp(jnp.sum(lp[2:3] * lp[3:4], axis=-1, keepdims=True)) + linit)
    def scores(i):
        lo = i * tq
        q = q_ref[lo:lo + tq, :]
        zero = jnp.zeros_like(q)
        return [_scores_t(jnp.where((lane < DIFF_HD) != bool(j), q, zero), k_ref, lo, bias_t)
                for j in range(2)]

    def outputs(i, sc):
        lo = i * tq
        accs = [_weighted_values_t(sc[j], vt_sc, lo) for j in range(2)]
        maps = [acc[0:dv, :] / acc[dv:dv + 1, :] for acc in accs]
        o_t = maps[0] - lam * maps[1]
        o_t = o_t * lax.rsqrt(jnp.mean(o_t * o_t, axis=0, keepdims=True) + EPS)
        o_ref[lo:lo + tq, :] = (o_t * (1.0 - linit)).T.astype(BF16)

    _skewed(s // tq, scores, outputs)


def _diff_attention(q, k, v, lam_params, lambda_init):
    b, s, _ = q.shape

    def spec():
        return pl.BlockSpec((None, s, LANES), lambda i, j: (i, 0, j))

    return pl.pallas_call(
        _diff_attn_kernel,
        grid=(b, DIFF_HEADS),
        in_specs=[spec(), spec(), spec(),
                  pl.BlockSpec((4, DIFF_HD), lambda i, j: (0, 0)),
                  pl.BlockSpec((1, 1), lambda i, j: (0, 0))],
        out_specs=spec(),
        out_shape=jax.ShapeDtypeStruct((b, s, BRANCH_WIDTH), BF16),
        scratch_shapes=[pltpu.VMEM((2 * DIFF_HD + ONES_ROWS, s), BF16)],
        compiler_params=_params("parallel", "parallel"),
        name="diff_attention",
    )(q, k, v, lam_params, jnp.full((1, 1), lambda_init, F32))


def _merge_kernel(x_ref, g_ref, yr_ref, ym_ref, yd_ref, wg_ref, wb_ref, wo_ref, o_ref):
    def pre_norm(n, _):
        return _rms(x_ref[_row_block(n), :], g_ref[0:1, :]).astype(BF16)

    def gated_sum(n, h):
        mixed = None
        for br, y_ref in enumerate((yr_ref, ym_ref, yd_ref)):
            gate = jax.nn.sigmoid(_dot_nt(h, wg_ref[br * D_MODEL:(br + 1) * D_MODEL, :]))
            term = gate * _dot(y_ref[_row_block(n), :], wb_ref[br])
            mixed = term if mixed is None else mixed + term
        return mixed.astype(BF16)

    def project(n, mixed):
        y = _dot(mixed, wo_ref[...])
        o_ref[_row_block(n), :] = x_ref[_row_block(n), :] + _rms(y, g_ref[1:2, :])

    _wavefront(x_ref.shape[0] // SUB_TILE, (pre_norm, gated_sum, project))


def _merge(x, gains, y_ret, y_mla, y_diff, wg, wb, wo, layer):
    t = x.shape[0]
    tm = DENSE_TILE

    def rows(w):
        return pl.BlockSpec((tm, w), lambda i: (i, 0))

    return pl.pallas_call(
        _merge_kernel,
        grid=(t // tm,),
        in_specs=[rows(D_MODEL),
                  _resident((2, D_MODEL), layer),
                  rows(BRANCH_WIDTH), rows(BRANCH_WIDTH), rows(BRANCH_WIDTH),
                  _resident((N_BRANCH * D_MODEL, D_MODEL), layer),
                  _resident((N_BRANCH, BRANCH_WIDTH, D_MODEL), layer),
                  _resident((D_MODEL, D_MODEL), layer)],
        out_specs=rows(D_MODEL),
        out_shape=jax.ShapeDtypeStruct((t, D_MODEL), F32),
        compiler_params=_params("parallel"),
        name="merge",
    )(x, gains, y_ret, y_mla, y_diff, wg, wb, wo)


def _memkv_kernel(mem_ref, g_ref, w_ref, o_ref):
    def pre_norm(n, _):
        return _rms(mem_ref[_row_block(n), :], g_ref[2:3, :]).astype(BF16)

    def project(n, m):
        o_ref[_row_block(n), :] = _dot(m, w_ref[...]).astype(BF16)

    _wavefront(mem_ref.shape[0] // SUB_TILE, (pre_norm, project))


def _memkv(mem, gains, wkv):
    b, n, _ = mem.shape
    depth = wkv.shape[0]
    out = pl.pallas_call(
        _memkv_kernel,
        grid=(depth,),
        in_specs=[pl.BlockSpec((b * n, D_MODEL), lambda l: (0, 0), pipeline_mode=pl.Buffered(1)),
                  pl.BlockSpec((None, 3, D_MODEL), lambda l: (l, 0, 0)),
                  pl.BlockSpec((None, D_MODEL, 2 * D_MODEL), lambda l: (l, 0, 0))],
        out_specs=pl.BlockSpec((None, b * n, 2 * D_MODEL), lambda l: (l, 0, 0)),
        out_shape=jax.ShapeDtypeStruct((depth, b * n, 2 * D_MODEL), BF16),
        compiler_params=_params("parallel"),
        name="memkv",
    )(mem.reshape(b * n, D_MODEL), gains, wkv)
    return out.reshape(depth, b, n, 2 * D_MODEL)


def _cross_kernel(x_ref, g_ref, kv_ref, wq_ref, wo_ref, o_ref):
    def query(n, _):
        h = _rms(x_ref[_row_block(n), :], g_ref[0:1, :]).astype(BF16)
        return (_dot(h, wq_ref[...]) * (LOG2E * CROSS_HD ** -0.5)).astype(BF16)

    def attend(n, q):
        heads = []
        for hd in range(CROSS_HEADS):
            cols = slice(hd * CROSS_HD, (hd + 1) * CROSS_HD)
            vcols = slice(D_MODEL + hd * CROSS_HD, D_MODEL + (hd + 1) * CROSS_HD)
            sc = _dot_nt(q[:, cols], kv_ref[:, cols])
            p = jnp.exp2(sc - jnp.max(sc, axis=-1, keepdims=True))
            total = jnp.sum(p, axis=-1, keepdims=True)
            heads.append((_dot(p.astype(BF16), kv_ref[:, vcols]) / total).astype(BF16))
        return jnp.concatenate(heads, axis=-1)

    def project(n, o):
        y = _dot(o, wo_ref[...])
        o_ref[_row_block(n), :] = x_ref[_row_block(n), :] + _rms(y, g_ref[1:2, :])

    _wavefront(x_ref.shape[0] // SUB_TILE, (query, attend, project))


def _cross(x, gains, kv, wq, wo, layer):
    b, s, _ = x.shape
    tm = DENSE_TILE
    n = kv.shape[2]
    return pl.pallas_call(
        _cross_kernel,
        grid=(b, s // tm),
        in_specs=[pl.BlockSpec((None, tm, D_MODEL), lambda i, j: (i, j, 0)),
                  _resident((3, D_MODEL), layer),
                  pl.BlockSpec((None, None, n, 2 * D_MODEL), lambda i, j: (layer, i, 0, 0)),
                  _resident((D_MODEL, D_MODEL), layer),
                  _resident((D_MODEL, D_MODEL), layer)],
        out_specs=pl.BlockSpec((None, tm, D_MODEL), lambda i, j: (i, j, 0)),
        out_shape=jax.ShapeDtypeStruct((b, s, D_MODEL), F32),
        compiler_params=_params("parallel", "parallel"),
        name="cross_attention",
    )(x, gains, kv, wq, wo)


K_ROPE_AT = 1920


def _pack_inproj(w_in, wq_b, wkv_b):
    depth, d, _ = w_in.shape
    w_t = jnp.swapaxes(w_in, 1, 2)
    wa = jnp.concatenate(
        [w_t[:, :K_ROPE_AT].astype(BF16),
         jnp.zeros((depth, MLA_NOPE, d), BF16),
         w_t[:, K_ROPE_AT:K_ROPE_AT + MLA_ROPE].astype(BF16),
         jnp.zeros((depth, LANES - MLA_NOPE - MLA_ROPE, d), BF16),
         w_t[:, K_ROPE_AT + MLA_ROPE:GATE_OFFSET].astype(BF16)], axis=1)
    wg = w_t[:, GATE_OFFSET:].astype(BF16)
    wq = wq_b.astype(BF16).reshape(depth, MLA_Q_RANK, MLA_HEADS, MLA_NOPE + MLA_ROPE)
    wq = jnp.pad(wq, ((0, 0), (0, 0), (0, 0), (0, LANES - MLA_NOPE - MLA_ROPE)))
    wq = wq.reshape(depth, MLA_Q_RANK, MLA_HEADS * LANES)
    wkv = wkv_b.astype(BF16).reshape(depth, MLA_KV_RANK, MLA_HEADS, MLA_NOPE + MLA_DV)
    wk = jnp.pad(wkv[..., :MLA_NOPE], ((0, 0), (0, 0), (0, 0), (0, LANES - MLA_NOPE)))
    wk = wk.reshape(depth, MLA_KV_RANK, MLA_HEADS * LANES)
    wv = wkv[..., MLA_NOPE:].reshape(depth, MLA_KV_RANK, MLA_HEADS * MLA_DV)
    return wa, wg, wq, wk, wv


def kernel(x, mem, positions, ffn1_norms, ffn1_w13, ffn1_w2, mix_norms, w_in, mla_q_norm,
           mla_kv_norm, mla_wq_b, mla_wkv_b, diff_lambda, w_branch, w_out, cross_norms,
           cross_wq, cross_wkv, cross_wo, ffn2_norms, ffn2_w13, ffn2_w2):
    b, s, d = x.shape
    t = b * s
    depth = w_in.shape[0]
    cos_tab, sin_tab = _rope_tables(positions)

    f1_w13, f1_w2 = ffn1_w13[0:1].astype(BF16), ffn1_w2[0:1].astype(BF16)
    wa, wg, wqb, wkbk, wkbv = _pack_inproj(w_in, mla_wq_b, mla_wkv_b)
    wb, wo = w_branch.astype(BF16), w_out.astype(BF16)
    c_wq, c_wo = cross_wq.astype(BF16), cross_wo.astype(BF16)
    q_norm = mla_q_norm.reshape(depth, 1, MLA_Q_RANK)
    kv_norm = mla_kv_norm.reshape(depth, 1, MLA_KV_RANK)

    mem_kv = _memkv(mem, cross_norms, cross_wkv.astype(BF16))

    def seq(a):
        return a.reshape(b, s, a.shape[-1])

    xf = x.reshape(t, d)
    for l in range(depth):
        lambda_init = 0.8 - 0.6 * math.exp(-0.3 * l)
        xf, f2_w13, f2_w2 = _ffn(xf, ffn1_norms, l, f1_w13, f1_w2,
                                 cast_next=(ffn2_w13, ffn2_w2, l))

        (rq, rk, rv, rg, mq, mk, mv, dq, dk, dv) = _inproj(
            xf, mix_norms, wa, q_norm, kv_norm, wqb, wkbk, wkbv, cos_tab, sin_tab, l)
        y_ret = _retention(seq(rq), seq(rk), seq(rv), seq(rg))
        y_mla = _mla_attention(seq(mq), seq(mk), seq(mv))
        y_diff = _diff_attention(seq(dq), seq(dk), seq(dv), diff_lambda[l], lambda_init)
        xf = _merge(xf, mix_norms, y_ret.reshape(t, -1), y_mla.reshape(t, -1),
                    y_diff.reshape(t, -1), wg, wb, wo, l)

        xf = _cross(xf.reshape(b, s, d), cross_norms, mem_kv, c_wq, c_wo, l).reshape(t, d)

        if l + 1 < depth:
            xf, f1_w13, f1_w2 = _ffn(xf, ffn2_norms, l, f2_w13, f2_w2,
                                     cast_next=(ffn1_w13, ffn1_w2, l + 1))
        else:
            xf = _ffn(xf, ffn2_norms, l, f2_w13, f2_w2)
    return xf.reshape(b, s, d)
```
